```python
import math
import jax, jax.numpy as jnp
from jax import lax
import numpy as np

D_MODEL = 1024
BATCH = 8
SEQ = 4096
DEPTH = 4

DA_HEADS = 4
DA_QK_DIM = 64
DA_V_DIM = 2 * DA_QK_DIM
SB_HEADS = 8
SB_DIM = 64
DIL_PAIRS = ((128, 1), (512, 4), (2048, 16))
DIL_GROUPS = len(DIL_PAIRS)
DIL_HEADS_PER_GROUP = 4
DIL_HEADS = DIL_GROUPS * DIL_HEADS_PER_GROUP
DIL_DIM = 128

N_BRANCHES = 3
Q_BLOCK = 128
D_FF = 4 * D_MODEL
PLE_DIM = 256
REL_BUCKETS = 32
REL_MAX_DIST = 2048
BIAS_HEADS = DA_HEADS + DIL_HEADS
DEEPNORM_ALPHA = (2 * DEPTH) ** 0.25
DEEPNORM_BETA = (8 * DEPTH) ** -0.25
LN_EPS = 1e-5
RMS_EPS = 1e-5

IN_SIZES = (
    DA_HEADS * 2 * DA_QK_DIM, DA_HEADS * 2 * DA_QK_DIM, DA_HEADS * DA_V_DIM,
    SB_HEADS * SB_DIM, SB_HEADS * SB_DIM, SB_HEADS * SB_DIM,
    DIL_HEADS * DIL_DIM, DIL_HEADS * DIL_DIM, DIL_HEADS * DIL_DIM,
    N_BRANCHES * D_MODEL,
)
VALUE_SLOTS = (2, 5, 8)
IN_COLS = sum(IN_SIZES)
DA_OUT = DA_HEADS * DA_V_DIM
SB_OUT = SB_HEADS * SB_DIM
DIL_OUT = DIL_HEADS_PER_GROUP * DIL_DIM

kernel_name = "hybrid_gated_diff_stickbreak_dilated"


def _split_points():
    return [int(c) for c in np.cumsum(IN_SIZES)[:-1]]


def layer_norm(x, g, b):
    xf = x.astype(jnp.float32)
    mu = xf.mean(-1, keepdims=True)
    var = jnp.square(xf - mu).mean(-1, keepdims=True)
    return ((xf - mu) * lax.rsqrt(var + LN_EPS) * g + b).astype(x.dtype)


def rel_bucket(dist):
    max_exact = REL_BUCKETS // 2
    d = jnp.maximum(dist, 0)
    log_ratio = jnp.log(jnp.maximum(d, 1).astype(jnp.float32) / max_exact) / math.log(REL_MAX_DIST / max_exact)
    large = max_exact + (log_ratio * (REL_BUCKETS - max_exact)).astype(jnp.int32)
    return jnp.where(d < max_exact, d, jnp.minimum(large, REL_BUCKETS - 1))


def diff_attention(q, k, v, lam, norm_g, bias_table, lam_init):
    B, S, H, _, d = q.shape
    nblk = S // Q_BLOCK
    scale = d ** -0.5
    q = q.transpose(0, 2, 3, 1, 4)
    k = k.transpose(0, 2, 3, 1, 4)
    vt = v.transpose(0, 2, 1, 3)
    qb = q.reshape(B, H, 2, nblk, Q_BLOCK, d).transpose(3, 0, 1, 2, 4, 5)
    k_pos = jnp.arange(S)

    def block(args):
        qi, i = args
        q_pos = i * Q_BLOCK + jnp.arange(Q_BLOCK)
        rel = q_pos[:, None] - k_pos[None, :]
        bias = bias_table[rel_bucket(rel)].transpose(2, 0, 1).astype(jnp.float32)
        logits = jnp.einsum('bhmqd,bhmkd->bhmqk', qi, k).astype(jnp.float32) * scale
        logits = jnp.where(rel >= 0, logits + bias[None, :, None], -jnp.inf)
        prob = jax.nn.softmax(logits, axis=-1)
        w = prob[:, :, 0] - lam * prob[:, :, 1]
        return jnp.einsum('bhqk,bhkd->bhqd', w.astype(vt.dtype), vt)

    o = lax.map(block, (qb, jnp.arange(nblk)))
    o = o.transpose(1, 0, 3, 2, 4).reshape(B, S, H, -1).astype(jnp.float32)
    o = o * lax.rsqrt(jnp.mean(jnp.square(o), axis=-1, keepdims=True) + RMS_EPS) * norm_g
    o = o * (1.0 - lam_init)
    return o.reshape(B, S, -1).astype(v.dtype)


def stick_breaking_attention(q, k, v):
    B, S, H, d = q.shape
    nblk = S // Q_BLOCK
    scale = d ** -0.5
    qb = q.reshape(B, nblk, Q_BLOCK, H, d).transpose(1, 0, 3, 2, 4)
    kt = k.transpose(0, 2, 1, 3)
    vt = v.transpose(0, 2, 1, 3)
    k_pos = jnp.arange(S)

    def block(args):
        qi, i = args
        q_pos = i * Q_BLOCK + jnp.arange(Q_BLOCK)
        before = k_pos[None, :] < q_pos[:, None]
        z = jnp.einsum('bhqd,bhkd->bhqk', qi, kt).astype(jnp.float32) * scale
        log_keep = jnp.where(before, jax.nn.log_sigmoid(-z), 0.0)
        later = lax.cumsum(log_keep, axis=3, reverse=True) - log_keep
        a = jnp.where(before, jnp.exp(jax.nn.log_sigmoid(z) + later), 0.0)
        return jnp.einsum('bhqk,bhkd->bhqd', a.astype(vt.dtype), vt)

    o = lax.map(block, (qb, jnp.arange(nblk)))
    return o.transpose(1, 0, 3, 2, 4).reshape(B, S, H * d)


def dilated_group(q, k, v, bias_table, window, dilation):
    B, S, H, dh = q.shape
    n = window // dilation
    L = -(-S // dilation)
    Lp = -(-L // n) * n
    nb = Lp // n

    def to_strided(a):
        a = jnp.pad(a, ((0, 0), (0, Lp * dilation - S), (0, 0), (0, 0)))
        a = a.reshape(B, Lp, dilation, H, dh).transpose(0, 2, 3, 1, 4)
        return a.reshape(B, dilation, H, nb, n, dh)

    def with_prev(a):
        prev = jnp.pad(a, ((0, 0), (0, 0), (0, 0), (1, 0), (0, 0), (0, 0)))[:, :, :, :-1]
        return jnp.concatenate([prev, a], axis=4)

    qs = to_strided(q)
    kb = with_prev(to_strided(k))
    vb = with_prev(to_strided(v))
    steps = jnp.arange(n)[:, None] + n - jnp.arange(2 * n)[None, :]
    valid_local = (steps >= 0) & (steps <= n)
    no_prev = (jnp.arange(nb)[:, None, None] == 0) & (jnp.arange(2 * n)[None, None, :] < n)
    valid = valid_local[None] & ~no_prev
    bias = bias_table[rel_bucket(steps * dilation)].transpose(2, 0, 1).astype(jnp.float32)
    logits = jnp.einsum('brhcqd,brhckd->brhcqk', qs, kb).astype(jnp.float32) * dh ** -0.5
    logits = jnp.where(valid, logits + bias[None, None, :, None], -jnp.inf)
    lse = jax.nn.logsumexp(logits, axis=-1)
    prob = jnp.exp(logits - lse[..., None])
    o = jnp.einsum('brhcqk,brhckd->brhcqd', prob.astype(vb.dtype), vb)
    o = o.reshape(B, dilation, H, Lp, dh).transpose(0, 3, 1, 2, 4).reshape(B, Lp * dilation, H, dh)[:, :S]
    lse = lse.reshape(B, dilation, H, Lp).transpose(0, 3, 1, 2).reshape(B, Lp * dilation, H)[:, :S]
    return o, lse


def dilated_attention(q, k, v, bias_table):
    B, S = q.shape[:2]
    outs, lses = [], []
    for g, (window, dilation) in enumerate(DIL_PAIRS):
        cols = bias_table[:, g * DIL_HEADS_PER_GROUP:(g + 1) * DIL_HEADS_PER_GROUP]
        o, lse = dilated_group(q[:, :, g], k[:, :, g], v[:, :, g], cols, window, dilation)
        outs.append(o)
        lses.append(lse)
    wts = jax.nn.softmax(jnp.stack(lses, axis=0), axis=0)
    o = wts[0][..., None] * outs[0] + wts[1][..., None] * outs[1] + wts[2][..., None] * outs[2]
    return o.reshape(B, S, -1).astype(q.dtype)


def mixer(x, w_in, da_lambda, da_norm, w_branch_da, w_branch_sb, w_branch_dil, w_out, rel_bias, lam_init):
    B, S, D = x.shape
    proj = x @ w_in
    (da_q, da_k, da_v, sb_q, sb_k, sb_v, dl_q, dl_k, dl_v, gates) = jnp.split(proj, _split_points(), axis=-1)
    lam_f = da_lambda.astype(jnp.float32)
    lam = jnp.exp(jnp.sum(lam_f[0] * lam_f[1])) - jnp.exp(jnp.sum(lam_f[2] * lam_f[3])) + lam_init
    o_da = diff_attention(da_q.reshape(B, S, DA_HEADS, 2, DA_QK_DIM),
                          da_k.reshape(B, S, DA_HEADS, 2, DA_QK_DIM),
                          da_v.reshape(B, S, DA_HEADS, DA_V_DIM),
                          lam, da_norm, rel_bias[:, :DA_HEADS], lam_init)
    o_sb = stick_breaking_attention(sb_q.reshape(B, S, SB_HEADS, SB_DIM),
                                    sb_k.reshape(B, S, SB_HEADS, SB_DIM),
                                    sb_v.reshape(B, S, SB_HEADS, SB_DIM))
    dil_shape = (B, S, DIL_GROUPS, DIL_HEADS_PER_GROUP, DIL_DIM)
    o_dl = dilated_attention(dl_q.reshape(dil_shape), dl_k.reshape(dil_shape), dl_v.reshape(dil_shape),
                             rel_bias[:, DA_HEADS:])
    g = jax.nn.sigmoid(gates).reshape(B, S, N_BRANCHES, D)
    merged = (g[:, :, 0] * (o_da @ w_branch_da) + g[:, :, 1] * (o_sb @ w_branch_sb)
              + g[:, :, 2] * (o_dl @ w_branch_dil))
    return merged @ w_out


def setup_inputs(seed: int = 0) -> dict:
    key = jax.random.key(seed)
    ks = jax.random.split(key, 20)
    f32 = jnp.float32
    nrm = lambda k, shape, s: jax.random.normal(k, shape, f32) * s
    col_scale = jnp.concatenate([jnp.full((s,), DEEPNORM_BETA if idx in VALUE_SLOTS else 1.0, f32)
                                 for idx, s in enumerate(IN_SIZES)])
    return {
        "x": nrm(ks[0], (BATCH, SEQ, D_MODEL), 1.0),
        "p": nrm(ks[1], (DEPTH, BATCH, SEQ, PLE_DIM), 1.0),
        "w_in": nrm(ks[2], (DEPTH, D_MODEL, IN_COLS), D_MODEL ** -0.5) * col_scale,
        "da_lambda": nrm(ks[3], (DEPTH, 4, DA_QK_DIM), 0.1),
        "da_norm": 1.0 + nrm(ks[4], (DEPTH, DA_V_DIM), 0.02),
        "w_branch_da": nrm(ks[5], (DEPTH, DA_OUT, D_MODEL), DA_OUT ** -0.5),
        "w_branch_sb": nrm(ks[6], (DEPTH, SB_OUT, D_MODEL), SB_OUT ** -0.5),
        "w_branch_dil": nrm(ks[7], (DEPTH, DIL_OUT, D_MODEL), DIL_OUT ** -0.5),
        "w_out": nrm(ks[8], (DEPTH, D_MODEL, D_MODEL), D_MODEL ** -0.5 * DEEPNORM_BETA),
        "ln1_g": 1.0 + nrm(ks[9], (DEPTH, D_MODEL), 0.02),
        "ln1_b": nrm(ks[10], (DEPTH, D_MODEL), 0.02),
        "w_up": nrm(ks[11], (DEPTH, D_MODEL, D_FF), D_MODEL ** -0.5),
        "w_down": nrm(ks[12], (DEPTH, D_FF, D_MODEL), D_FF ** -0.5 * DEEPNORM_BETA),
        "w_ple_gate": nrm(ks[13], (DEPTH, D_MODEL, D_MODEL), D_MODEL ** -0.5),
        "w_ple": nrm(ks[14], (DEPTH, PLE_DIM, D_MODEL), PLE_DIM ** -0.5 * DEEPNORM_BETA),
        "ln2_g": 1.0 + nrm(ks[15], (DEPTH, D_MODEL), 0.02),
        "ln2_b": nrm(ks[16], (DEPTH, D_MODEL), 0.02),
        "rel_bias": nrm(ks[17], (REL_BUCKETS, BIAS_HEADS), 0.3),
    }


def reference(x, p, w_in, da_lambda, da_norm, w_branch_da, w_branch_sb, w_branch_dil, w_out,
              ln1_g, ln1_b, w_up, w_down, w_ple_gate, w_ple, ln2_g, ln2_b, rel_bias):
    for i in range(DEPTH):
        lam_init = 0.8 - 0.6 * math.exp(-0.3 * i)
        m = mixer(x, w_in[i], da_lambda[i], da_norm[i], w_branch_da[i], w_branch_sb[i],
                  w_branch_dil[i], w_out[i], rel_bias, lam_init)
        x = layer_norm(DEEPNORM_ALPHA * x + m, ln1_g[i], ln1_b[i])
        hid = jax.nn.relu(x @ w_up[i])
        c = jnp.square(hid) @ w_down[i]
        ple = jax.nn.sigmoid(x @ w_ple_gate[i]) * (p[i] @ w_ple[i])
        x = layer_norm(DEEPNORM_ALPHA * x + c + ple, ln2_g[i], ln2_b[i])
    return x
```

```python
import functools
import math

import jax
import jax.numpy as jnp
from jax import lax
from jax.experimental import pallas as pl
from jax.experimental.pallas import tpu as pltpu

D_MODEL = 1024
DA_HEADS = 4
DA_QK_DIM = 64
DA_V_DIM = 2 * DA_QK_DIM
SB_HEADS = 8
SB_DIM = 64
DIL_PAIRS = ((128, 1), (512, 4), (2048, 16))
DIL_GROUPS = len(DIL_PAIRS)
DIL_HEADS_PER_GROUP = 4
DIL_DIM = 128
DIL_KEYS = 128
D_FF = 4 * D_MODEL
PLE_DIM = 256
REL_BUCKETS = 32
REL_MAX_DIST = 2048
BIAS_HEADS = DA_HEADS + DIL_GROUPS * DIL_HEADS_PER_GROUP
LN_EPS = 1e-5
RMS_EPS = 1e-5

DA_COLS = DA_HEADS * 2 * DA_QK_DIM
SB_COLS = SB_HEADS * SB_DIM
DIL_COLS = DIL_HEADS_PER_GROUP * DIL_DIM
GATE_COLS = 3 * D_MODEL

LANES = 128
VMEM_LIMIT = 48 * 1024 * 1024

ATT_TILE = 256
PROJ_TM = 1024
MERGE_TM = 256
MLP_TM = 512
MLP_TF = 1024
DIL_CHUNK = 1024

F32 = jnp.float32
BF16 = jnp.bfloat16
NEG_INF = float("-inf")


def _cparams(sem):
    return pltpu.CompilerParams(dimension_semantics=("arbitrary",) * len(sem), vmem_limit_bytes=VMEM_LIMIT)


def _dot(a, b):
    return jnp.dot(a, b, preferred_element_type=F32)


def _dot_nt(a, b):
    return lax.dot_general(a, b, (((1,), (1,)), ((), ())), preferred_element_type=F32)


def _sigmoid(v):
    return 1.0 / (1.0 + jnp.exp(-v))


def _layer_norm(y, g, b):
    mu = jnp.mean(y, axis=-1, keepdims=True)
    yc = y - mu
    var = jnp.mean(yc * yc, axis=-1, keepdims=True)
    return yc * lax.rsqrt(var + LN_EPS) * g + b


def _rel_bucket(dist):
    max_exact = REL_BUCKETS // 2
    d = jnp.maximum(dist, 0)
    log_ratio = jnp.log(jnp.maximum(d, 1).astype(F32) / max_exact) / math.log(REL_MAX_DIST / max_exact)
    large = max_exact + (log_ratio * (REL_BUCKETS - max_exact)).astype(jnp.int32)
    return jnp.where(d < max_exact, d, jnp.minimum(large, REL_BUCKETS - 1))


def _bias_lookup(d, head, thr_ref, tab_ref):
    val = jnp.full(d.shape, tab_ref[0, head], F32)
    for k in range(1, REL_BUCKETS):
        val = jnp.where(d >= thr_ref[k], tab_ref[k, head], val)
    return val


def _da_bias_kernel(thr_ref, tab_ref, o_ref, *, tile):
    h = pl.program_id(0)
    delta = pl.program_id(1)
    row = lax.broadcasted_iota(jnp.int32, (tile, tile), 0)
    col = lax.broadcasted_iota(jnp.int32, (tile, tile), 1)
    d = delta * tile + row - col
    val = _bias_lookup(d, h, thr_ref, tab_ref)
    o_ref[0, 0] = jnp.where(d >= 0, val, NEG_INF)


def _da_bias_tiles(thr, rel_bias, seq, tile):
    nd = seq // tile
    return pl.pallas_call(
        functools.partial(_da_bias_kernel, tile=tile),
        grid=(DA_HEADS, nd),
        in_specs=[pl.BlockSpec(memory_space=pltpu.SMEM), pl.BlockSpec(memory_space=pltpu.SMEM)],
        out_specs=pl.BlockSpec((1, 1, tile, tile), lambda h, d: (h, d, 0, 0)),
        out_shape=jax.ShapeDtypeStruct((DA_HEADS, nd, tile, tile), F32),
        compiler_params=_cparams(("arbitrary", "arbitrary")),
        name="da_bias_tiles",
    )(thr, rel_bias)


def _dil_bias_kernel(thr_ref, tab_ref, o_ref):
    g = pl.program_id(0)
    hs = pl.program_id(1)
    n = DIL_KEYS
    dil = jnp.where(g == 0, DIL_PAIRS[0][1], jnp.where(g == 1, DIL_PAIRS[1][1], DIL_PAIRS[2][1]))
    row = lax.broadcasted_iota(jnp.int32, (n, 2 * n), 0)
    col = lax.broadcasted_iota(jnp.int32, (n, 2 * n), 1)
    step = row + n - col
    head = DA_HEADS + g * DIL_HEADS_PER_GROUP + hs
    val = _bias_lookup(step * dil, head, thr_ref, tab_ref)
    valid = jnp.where(step >= 0, jnp.where(step <= n, 1, 0), 0)
    o_ref[0, 0, 0] = jnp.where(valid > 0, val, NEG_INF)
    o_ref[0, 0, 1] = jnp.where(jnp.where(col >= n, valid, 0) > 0, val, NEG_INF)


def _dil_bias_tiles(thr, rel_bias):
    n = DIL_KEYS
    return pl.pallas_call(
        _dil_bias_kernel,
        grid=(DIL_GROUPS, DIL_HEADS_PER_GROUP),
        in_specs=[pl.BlockSpec(memory_space=pltpu.SMEM), pl.BlockSpec(memory_space=pltpu.SMEM)],
        out_specs=pl.BlockSpec((1, 1, 2, n, 2 * n), lambda g, h: (g, h, 0, 0, 0)),
        out_shape=jax.ShapeDtypeStruct((DIL_GROUPS, DIL_HEADS_PER_GROUP, 2, n, 2 * n), F32),
        compiler_params=_cparams(("arbitrary", "arbitrary")),
        name="dil_bias_tiles",
    )(thr, rel_bias)


def _proj_kernel(x_ref, w_ref, o_ref):
    o_ref[0] = _dot(x_ref[0], w_ref[...]).astype(o_ref.dtype)


def _proj(xb, w, tn):
    bsz, seq, d = xb.shape
    c = w.shape[1]
    tm = min(PROJ_TM, seq)
    return pl.pallas_call(
        _proj_kernel,
        grid=(bsz, seq // tm, c // tn),
        in_specs=[pl.BlockSpec((1, tm, d), lambda b, i, j: (b, i, 0)),
                  pl.BlockSpec((d, tn), lambda b, i, j: (0, j))],
        out_specs=pl.BlockSpec((1, tm, tn), lambda b, i, j: (b, i, j)),
        out_shape=jax.ShapeDtypeStruct((bsz, seq, c), BF16),
        compiler_params=_cparams(("parallel", "parallel", "arbitrary")),
        name="proj",
    )(xb, w)


def _proj_dil_kernel(x_ref, w_ref, o_ref, res_ref, *, dil, rows):
    res = _dot(x_ref[0], w_ref[...])
    nblk = res.shape[1] // LANES
    for c in range(nblk):
        res_ref[c] = res[:, c * LANES:(c + 1) * LANES]
    for r in range(dil):
        for c in range(nblk):
            o_ref[0, r, :, c * LANES:(c + 1) * LANES] = (
                res_ref[c, pl.ds(r, rows, stride=dil), :].astype(o_ref.dtype))


def _proj_dil(xb, w, dil):
    bsz, seq, d = xb.shape
    c = w.shape[1]
    tm = min(PROJ_TM, seq)
    rows = tm // dil
    return pl.pallas_call(
        functools.partial(_proj_dil_kernel, dil=dil, rows=rows),
        grid=(bsz, seq // tm),
        in_specs=[pl.BlockSpec((1, tm, d), lambda b, i: (b, i, 0)),
                  pl.BlockSpec((d, c), lambda b, i: (0, 0))],
        out_specs=pl.BlockSpec((1, dil, rows, c), lambda b, i: (b, 0, i, 0)),
        out_shape=jax.ShapeDtypeStruct((bsz, dil, seq // dil, c), BF16),
        scratch_shapes=[pltpu.VMEM((c // LANES, tm, LANES), F32)],
        compiler_params=_cparams(("parallel", "arbitrary")),
        name="proj_dil",
    )(xb, w)


def _da_kernel(q_ref, k_ref, v_ref, bias_ref, lam_ref, g_ref, o_ref, *, tile, lam_init):
    qi = pl.program_id(2)
    lane = lax.broadcasted_iota(jnp.int32, (tile, LANES), 1)
    qs = q_ref[0] * jnp.asarray(DA_QK_DIM ** -0.5, BF16)
    zero = jnp.zeros_like(qs)
    q_maps = (jnp.where(lane < DA_QK_DIM, qs, zero), jnp.where(lane >= DA_QK_DIM, qs, zero))

    def step(ki, carry):
        start = pl.multiple_of(ki * tile, tile)
        k = k_ref[0, pl.ds(start, tile), :]
        v = v_ref[0, pl.ds(start, tile), :]
        bias = bias_ref[0, qi - ki]
        out = []
        for mp in range(2):
            m, l, acc = carry[mp]
            s = _dot_nt(q_maps[mp], k) + bias
            m_new = jnp.maximum(m, jnp.max(s, axis=-1, keepdims=True))
            alpha = jnp.exp(m - m_new)
            p = jnp.exp(s - m_new)
            l = alpha * l + jnp.sum(p, axis=-1, keepdims=True)
            acc = alpha * acc + _dot(p.astype(BF16), v)
            out.append((m_new, l, acc))
        return tuple(out)

    init = tuple((jnp.full((tile, 1), NEG_INF, F32), jnp.zeros((tile, 1), F32),
                  jnp.zeros((tile, DA_V_DIM), F32)) for _ in range(2))
    (_, l1, a1), (_, l2, a2) = lax.fori_loop(0, qi + 1, step, init)

    ll = lam_ref[...]
    lam = (jnp.exp(jnp.sum(ll[0:1] * ll[1:2], axis=-1, keepdims=True))
           - jnp.exp(jnp.sum(ll[2:3] * ll[3:4], axis=-1, keepdims=True)) + lam_init)
    o = a1 * (1.0 / l1) - lam * (a2 * (1.0 / l2))
    o = o * lax.rsqrt(jnp.mean(o * o, axis=-1, keepdims=True) + RMS_EPS) * g_ref[...]
    o_ref[0] = (o * (1.0 - lam_init)).astype(o_ref.dtype)


def _da_attention(proj, bias, lam_p, norm_g, lam_init):
    bsz, seq, _ = proj.shape
    tile = bias.shape[-1]
    nq = seq // tile
    kb = DA_COLS // LANES
    return pl.pallas_call(
        functools.partial(_da_kernel, tile=tile, lam_init=lam_init),
        grid=(DA_HEADS, bsz, nq),
        in_specs=[pl.BlockSpec((1, tile, LANES), lambda h, b, i: (b, i, h)),
                  pl.BlockSpec((1, seq, LANES), lambda h, b, i: (b, 0, kb + h)),
                  pl.BlockSpec((1, seq, LANES), lambda h, b, i: (b, 0, 2 * kb + h)),
                  pl.BlockSpec((1, nq, tile, tile), lambda h, b, i: (h, 0, 0, 0)),
                  pl.BlockSpec((4, DA_QK_DIM), lambda h, b, i: (0, 0)),
                  pl.BlockSpec((1, DA_V_DIM), lambda h, b, i: (0, 0))],
        out_specs=pl.BlockSpec((1, tile, LANES), lambda h, b, i: (b, i, h)),
        out_shape=jax.ShapeDtypeStruct((bsz, seq, DA_COLS), BF16),
        compiler_params=_cparams(("arbitrary", "parallel", "arbitrary")),
        name="diff_attention",
    )(proj, proj, proj, bias, lam_p, norm_g)


def _sb_kernel(q_ref, k_ref, v_ref, o_ref, *, tile):
    qi = pl.program_id(2)
    lane = lax.broadcasted_iota(jnp.int32, (tile, LANES), 1)
    qs = q_ref[0] * jnp.asarray(SB_DIM ** -0.5, BF16)
    zero = jnp.zeros_like(qs)
    head_lanes = (lane < SB_DIM, lane >= SB_DIM)
    q_heads = tuple(jnp.where(hl, qs, zero) for hl in head_lanes)
    row = lax.broadcasted_iota(jnp.int32, (tile, tile), 0)
    col = lax.broadcasted_iota(jnp.int32, (tile, tile), 1)
    before = row > col
    tri = jnp.where(before, 1.0, 0.0).astype(BF16)

    def tile_step(ki, carry, diag):
        cs, acc = carry
        start = pl.multiple_of(ki * tile, tile)
        k = k_ref[0, pl.ds(start, tile), :]
        v = v_ref[0, pl.ds(start, tile), :]
        new_cs = []
        for h in range(2):
            z = _dot_nt(q_heads[h], k)
            soft = jnp.log(1.0 + jnp.exp(-jnp.abs(z)))
            log_beta = jnp.minimum(z, 0.0) - soft
            log_keep = log_beta - z
            if diag:
                log_keep = jnp.where(before, log_keep, 0.0)
            hi = log_keep.astype(BF16)
            lo = (log_keep - hi.astype(F32)).astype(BF16)
            later = _dot(hi, tri) + _dot(lo, tri) + cs[h]
            a = jnp.exp(log_beta + later)
            if diag:
                a = jnp.where(before, a, 0.0)
            new_cs.append(cs[h] + jnp.sum(log_keep, axis=-1, keepdims=True))
            acc = acc + _dot(a.astype(BF16), jnp.where(head_lanes[h], v, jnp.zeros_like(v)))
        return tuple(new_cs), acc

    carry = ((jnp.zeros((tile, 1), F32), jnp.zeros((tile, 1), F32)), jnp.zeros((tile, LANES), F32))
    carry = tile_step(qi, carry, True)
    carry = lax.fori_loop(0, qi, lambda it, c: tile_step(qi - 1 - it, c, False), carry)
    o_ref[0] = carry[1].astype(o_ref.dtype)


def _sb_attention(proj):
    bsz, seq, _ = proj.shape
    tile = min(ATT_TILE, seq)
    nq = seq // tile
    qb = 3 * DA_COLS // LANES
    kb = qb + SB_COLS // LANES
    vb = kb + SB_COLS // LANES
    return pl.pallas_call(
        functools.partial(_sb_kernel, tile=tile),
        grid=(bsz, SB_COLS // LANES, nq),
        in_specs=[pl.BlockSpec((1, tile, LANES), lambda b, h, i: (b, i, qb + h)),
                  pl.BlockSpec((1, seq, LANES), lambda b, h, i: (b, 0, kb + h)),
                  pl.BlockSpec((1, seq, LANES), lambda b, h, i: (b, 0, vb + h))],
        out_specs=pl.BlockSpec((1, tile, LANES), lambda b, h, i: (b, i, h)),
        out_shape=jax.ShapeDtypeStruct((bsz, seq, SB_COLS), BF16),
        compiler_params=_cparams(("parallel", "parallel", "arbitrary")),
        name="stick_breaking_attention",
    )(proj, proj, proj)


def _dil_kernel(q_ref, kc_ref, kp_ref, vc_ref, vp_ref, bias_ref, o_ref, lse_ref, *, nblk):
    cc = pl.program_id(2)
    n = DIL_KEYS
    scale = DIL_DIM ** -0.5
    first = jnp.where(cc == 0, 1, 0)

    def block(qb, kp, kc, vp, vc, bias):
        sp = _dot_nt(qb, kp) * scale + bias[:, :n]
        sc = _dot_nt(qb, kc) * scale + bias[:, n:]
        m = jnp.maximum(jnp.max(sp, axis=-1, keepdims=True), jnp.max(sc, axis=-1, keepdims=True))
        pp = jnp.exp(sp - m)
        pc = jnp.exp(sc - m)
        l = jnp.sum(pp, axis=-1, keepdims=True) + jnp.sum(pc, axis=-1, keepdims=True)
        o = _dot(pp.astype(BF16), vp) + _dot(pc.astype(BF16), vc)
        return o * (1.0 / l), m + jnp.log(l)

    for hs in range(DIL_HEADS_PER_GROUP):
        cols = slice(hs * DIL_DIM, (hs + 1) * DIL_DIM)
        o, lse = block(q_ref[0, 0, 0:n, cols], kp_ref[0, 0, :, cols], kc_ref[0, 0, 0:n, cols],
                       vp_ref[0, 0, :, cols], vc_ref[0, 0, 0:n, cols], bias_ref[0, hs, first])
        o_ref[0, 0, 0:n, cols] = o
        lse_ref[0, 0, 0:n, cols] = jnp.broadcast_to(lse, (n, DIL_DIM))

        def body(j, _, cols=cols, hs=hs):
            cur = pl.ds(pl.multiple_of(j * n, n), n)
            prev = pl.ds(pl.multiple_of((j - 1) * n, n), n)
            o, lse = block(q_ref[0, 0, cur, cols], kc_ref[0, 0, prev, cols], kc_ref[0, 0, cur, cols],
                           vc_ref[0, 0, prev, cols], vc_ref[0, 0, cur, cols], bias_ref[0, hs, 0])
            o_ref[0, 0, cur, cols] = o
            lse_ref[0, 0, cur, cols] = jnp.broadcast_to(lse, (n, DIL_DIM))
            return 0

        lax.fori_loop(1, nblk, body, 0)


def _dil_attention(qkv, col_blk, bias, group):
    bsz, dil, length, _ = qkv.shape
    n = DIL_KEYS
    chunk = min(DIL_CHUNK, length)
    nblk = chunk // n
    w = DIL_COLS
    cur = lambda off: pl.BlockSpec((1, 1, chunk, w), lambda b, r, c: (b, r, c, col_blk + off))
    prev = lambda off: pl.BlockSpec((1, 1, n, w),
                                    lambda b, r, c: (b, r, jnp.maximum(c * nblk - 1, 0), col_blk + off))
    out_spec = pl.BlockSpec((1, 1, chunk, w), lambda b, r, c: (b, r, c, 0))
    shape = jax.ShapeDtypeStruct((bsz, dil, length, w), F32)
    return pl.pallas_call(
        functools.partial(_dil_kernel, nblk=nblk),
        grid=(bsz, dil, length // chunk),
        in_specs=[cur(0), cur(1), prev(1), cur(2), prev(2),
                  pl.BlockSpec((1, DIL_HEADS_PER_GROUP, 2, n, 2 * n), lambda b, r, c: (group, 0, 0, 0, 0))],
        out_specs=[out_spec, out_spec],
        out_shape=[shape, shape],
        compiler_params=_cparams(("parallel", "parallel", "arbitrary")),
        name="dilated_attention",
    )(qkv, qkv, qkv, qkv, qkv, bias)


def _merge_kernel(x_ref, xb_ref, oa_ref, ob_ref, o0_ref, l0_ref, o1_ref, l1_ref, o2_ref, l2_ref,
                  wa_ref, wb_ref, wc_ref, wg_ref, wo_ref, g_ref, b_ref,
                  y_ref, yb_ref, s1o, s1l, s2o, s2l, oc_ref, *, alpha, tm):
    for src, dst in ((o1_ref, s1o), (l1_ref, s1l), (o2_ref, s2o), (l2_ref, s2l)):
        dil = src.shape[1]
        rows = tm // dil
        for r in range(dil):
            for c in range(DIL_HEADS_PER_GROUP):
                dst[c, pl.ds(r, rows, stride=dil), :] = src[0, r, :, c * DIL_DIM:(c + 1) * DIL_DIM]
    for c in range(DIL_HEADS_PER_GROUP):
        cols = slice(c * DIL_DIM, (c + 1) * DIL_DIM)
        l0 = l0_ref[0, 0, :, cols]
        l1 = s1l[c]
        l2 = s2l[c]
        mx = jnp.maximum(jnp.maximum(l0, l1), l2)
        w0 = jnp.exp(l0 - mx)
        w1 = jnp.exp(l1 - mx)
        w2 = jnp.exp(l2 - mx)
        oc = (w0 * o0_ref[0, 0, :, cols] + w1 * s1o[c] + w2 * s2o[c]) * (1.0 / (w0 + w1 + w2))
        oc_ref[:, cols] = oc.astype(BF16)

    gates = _sigmoid(_dot(xb_ref[...], wg_ref[...]))
    d = D_MODEL
    merged = (gates[:, 0:d] * _dot(oa_ref[...], wa_ref[...])
              + gates[:, d:2 * d] * _dot(ob_ref[...], wb_ref[...])
              + gates[:, 2 * d:3 * d] * _dot(oc_ref[...], wc_ref[...]))
    mixed = _dot(merged.astype(BF16), wo_ref[...])
    y = _layer_norm(alpha * x_ref[...] + mixed, g_ref[...], b_ref[...])
    y_ref[...] = y
    yb_ref[...] = y.astype(BF16)


def _merge(x, xb, oa, ob, c0, c1, c2, wa, wb, wc, wg, wo, g, b, alpha):
    bsz, seq, d = x.shape
    tm = min(MERGE_TM, seq)
    row = lambda width: pl.BlockSpec((None, tm, width), lambda bb, i: (bb, i, 0))
    full = lambda arr: pl.BlockSpec(arr.shape, lambda bb, i: (0,) * arr.ndim)

    def strided(arr):
        dil = arr.shape[1]
        return pl.BlockSpec((1, dil, tm // dil, DIL_COLS), lambda bb, i: (bb, 0, i, 0))

    weights = (wa, wb, wc, wg, wo, g, b)
    return pl.pallas_call(
        functools.partial(_merge_kernel, alpha=alpha, tm=tm),
        grid=(bsz, seq // tm),
        in_specs=[row(d), row(d), row(DA_COLS), row(SB_COLS),
                  strided(c0[0]), strided(c0[1]), strided(c1[0]), strided(c1[1]),
                  strided(c2[0]), strided(c2[1])] + [full(a) for a in weights],
        out_specs=[row(d), row(d)],
        out_shape=[jax.ShapeDtypeStruct((bsz, seq, d), F32), jax.ShapeDtypeStruct((bsz, seq, d), BF16)],
        scratch_shapes=[pltpu.VMEM((DIL_HEADS_PER_GROUP, tm, DIL_DIM), F32) for _ in range(4)]
        + [pltpu.VMEM((tm, DIL_COLS), BF16)],
        compiler_params=_cparams(("parallel", "arbitrary")),
        name="merge_outproj_ln",
    )(x, xb, oa, ob, c0[0], c0[1], c1[0], c1[1], c2[0], c2[1], *weights)


def _mlp_kernel(x_ref, xb_ref, p_ref, wu_ref, wd_ref, wpg_ref, wp_ref, g_ref, b_ref,
                y_ref, yb_ref, acc_ref, *, alpha):
    f = pl.program_id(1)
    hid = jnp.maximum(_dot(xb_ref[...], wu_ref[...]), 0.0)
    part = _dot((hid * hid).astype(BF16), wd_ref[...])

    @pl.when(f == 0)
    def _():
        acc_ref[...] = part

    @pl.when(f > 0)
    def _():
        acc_ref[...] += part

    @pl.when(f == pl.num_programs(1) - 1)
    def _():
        ple = _sigmoid(_dot(xb_ref[...], wpg_ref[...])) * _dot(p_ref[...].astype(BF16), wp_ref[...])
        y = _layer_norm(alpha * x_ref[...] + acc_ref[...] + ple, g_ref[...], b_ref[...])
        y_ref[...] = y
        yb_ref[...] = y.astype(BF16)


def _mlp(x, xb, p, wu, wd, wpg, wp, g, b, alpha):
    n, d = x.shape
    tm = min(MLP_TM, n)
    tf = MLP_TF
    row = lambda width: pl.BlockSpec((tm, width), lambda i, f: (i, 0))
    full = lambda arr: pl.BlockSpec(arr.shape, lambda i, f: (0,) * arr.ndim)
    return pl.pallas_call(
        functools.partial(_mlp_kernel, alpha=alpha),
        grid=(n // tm, D_FF // tf),
        in_specs=[row(d), row(d), row(PLE_DIM),
                  pl.BlockSpec((d, tf), lambda i, f: (0, f)),
                  pl.BlockSpec((tf, d), lambda i, f: (f, 0)),
                  full(wpg), full(wp), full(g), full(b)],
        out_specs=[row(d), row(d)],
        out_shape=[jax.ShapeDtypeStruct((n, d), F32), jax.ShapeDtypeStruct((n, d), BF16)],
        scratch_shapes=[pltpu.VMEM((tm, d), F32)],
        compiler_params=_cparams(("parallel", "arbitrary")),
        name="mlp_ple_ln",
    )(x, xb, p, wu, wd, wpg, wp, g, b)


def kernel(x, p, w_in, da_lambda, da_norm, w_branch_da, w_branch_sb, w_branch_dil, w_out,
           ln1_g, ln1_b, w_up, w_down, w_ple_gate, w_ple, ln2_g, ln2_b, rel_bias):
    bsz, seq, d = x.shape
    depth = w_in.shape[0]
    n = bsz * seq
    alpha = (2 * depth) ** 0.25
    assert d == D_MODEL and all(wd // dl == DIL_KEYS for wd, dl in DIL_PAIRS)
    assert seq % (DIL_KEYS * DIL_PAIRS[-1][1]) == 0 and seq % min(ATT_TILE, seq) == 0

    qkv_a = 3 * DA_COLS
    qkv_b = 3 * SB_COLS
    c0 = qkv_a + qkv_b
    cw = DIL_GROUPS * DIL_COLS

    def dil_cols(g):
        return [w_in[:, :, c0 + t * cw + g * DIL_COLS: c0 + t * cw + (g + 1) * DIL_COLS] for t in range(3)]

    w_main = jnp.concatenate([w_in[:, :, :c0]] + dil_cols(0), axis=-1).astype(BF16)
    w_g1 = jnp.concatenate(dil_cols(1), axis=-1).astype(BF16)
    w_g2 = jnp.concatenate(dil_cols(2), axis=-1).astype(BF16)
    w_gate = w_in[:, :, c0 + 3 * cw:].astype(BF16)
    wa, wb, wc, wo = (w.astype(BF16) for w in (w_branch_da, w_branch_sb, w_branch_dil, w_out))
    wu, wd, wpg, wp = (w.astype(BF16) for w in (w_up, w_down, w_ple_gate, w_ple))

    dmax = max(seq, REL_MAX_DIST + 1)
    buckets = _rel_bucket(jnp.arange(dmax, dtype=jnp.int32))
    thr = jnp.sum(buckets[None, :] < jnp.arange(REL_BUCKETS, dtype=jnp.int32)[:, None], axis=1).astype(jnp.int32)
    tile = min(ATT_TILE, seq)
    da_bias = _da_bias_tiles(thr, rel_bias, seq, tile)
    dil_bias = _dil_bias_tiles(thr, rel_bias)

    xb = x.astype(BF16)
    for i in range(depth):
        lam_init = 0.8 - 0.6 * math.exp(-0.3 * i)
        proj = _proj(xb, w_main[i], 3 * DA_COLS)
        proj_g1 = _proj_dil(xb, w_g1[i], DIL_PAIRS[1][1])
        proj_g2 = _proj_dil(xb, w_g2[i], DIL_PAIRS[2][1])
        oa = _da_attention(proj, da_bias, da_lambda[i], da_norm[i].reshape(1, DA_V_DIM), lam_init)
        ob = _sb_attention(proj)
        cgrp0 = _dil_attention(proj.reshape(bsz, 1, seq, proj.shape[-1]), c0 // DIL_COLS, dil_bias, 0)
        cgrp1 = _dil_attention(proj_g1, 0, dil_bias, 1)
        cgrp2 = _dil_attention(proj_g2, 0, dil_bias, 2)
        x1, x1b = _merge(x, xb, oa, ob, cgrp0, cgrp1, cgrp2, wa[i], wb[i], wc[i], w_gate[i], wo[i],
                         ln1_g[i].reshape(1, d), ln1_b[i].reshape(1, d), alpha)
        x2, x2b = _mlp(x1.reshape(n, d), x1b.reshape(n, d), p[i].reshape(n, PLE_DIM), wu[i], wd[i], wpg[i], wp[i],
                       ln2_g[i].reshape(1, d), ln2_b[i].reshape(1, d), alpha)
        x = x2.reshape(bsz, seq, d)
        xb = x2b.reshape(bsz, seq, d)
    return x
```

```python
import functools
import math

import jax
import jax.numpy as jnp
from jax import lax
from jax.experimental import pallas as pl
from jax.experimental.pallas import tpu as pltpu

D_MODEL = 1024
DA_HEADS = 4
DA_QK_DIM = 64
DA_V_DIM = 2 * DA_QK_DIM
SB_HEADS = 8
SB_DIM = 64
DIL_PAIRS = ((128, 1), (512, 4), (2048, 16))
DIL_GROUPS = len(DIL_PAIRS)
DIL_HEADS_PER_GROUP = 4
DIL_DIM = 128
DIL_KEYS = 128
D_FF = 4 * D_MODEL
PLE_DIM = 256
REL_BUCKETS = 32
REL_MAX_DIST = 2048
BIAS_HEADS = DA_HEADS + DIL_GROUPS * DIL_HEADS_PER_GROUP
LN_EPS = 1e-5
RMS_EPS = 1e-5

DA_COLS = DA_HEADS * 2 * DA_QK_DIM
SB_COLS = SB_HEADS * SB_DIM
DIL_COLS = DIL_HEADS_PER_GROUP * DIL_DIM
GATE_COLS = 3 * D_MODEL

LANES = 128
VMEM_LIMIT = 48 * 1024 * 1024

ATT_TILE = 256
DA_TQ = 2 * ATT_TILE
PROJ_TM = 1024
MERGE_TM = 256
MLP_TM = 512
MLP_TF = 1024
DIL_CHUNK = 1024

F32 = jnp.float32
BF16 = jnp.bfloat16
NEG_INF = float("-inf")


def _cparams(sem):
    return pltpu.CompilerParams(dimension_semantics=("arbitrary",) * len(sem), vmem_limit_bytes=VMEM_LIMIT)


def _dot(a, b):
    return jnp.dot(a, b, preferred_element_type=F32)


def _dot_nt(a, b):
    return lax.dot_general(a, b, (((1,), (1,)), ((), ())), preferred_element_type=F32)


def _sigmoid(v):
    return 1.0 / (1.0 + jnp.exp(-v))


def _layer_norm(y, g, b):
    mu = jnp.mean(y, axis=-1, keepdims=True)
    yc = y - mu
    var = jnp.mean(yc * yc, axis=-1, keepdims=True)
    return yc * lax.rsqrt(var + LN_EPS) * g + b


def _rel_bucket(dist):
    max_exact = REL_BUCKETS // 2
    d = jnp.maximum(dist, 0)
    log_ratio = jnp.log(jnp.maximum(d, 1).astype(F32) / max_exact) / math.log(REL_MAX_DIST / max_exact)
    large = max_exact + (log_ratio * (REL_BUCKETS - max_exact)).astype(jnp.int32)
    return jnp.where(d < max_exact, d, jnp.minimum(large, REL_BUCKETS - 1))


def _bias_lookup(d, head, thr_ref, tab_ref):
    val = jnp.full(d.shape, tab_ref[0, head], F32)
    for k in range(1, REL_BUCKETS):
        val = jnp.where(d >= thr_ref[k], tab_ref[k, head], val)
    return val


def _da_bias_kernel(thr_ref, tab_ref, o_ref, *, tk, tq):
    h = pl.program_id(0)
    delta = pl.program_id(1) - 1
    key = lax.broadcasted_iota(jnp.int32, (tk, tq), 0)
    query = lax.broadcasted_iota(jnp.int32, (tk, tq), 1)
    d = delta * tk + query - key
    val = _bias_lookup(d, h, thr_ref, tab_ref)
    o_ref[0, 0] = jnp.where(d >= 0, val, NEG_INF)


def _da_bias_tiles(thr, rel_bias, seq, tk, tq):
    assert tq == 2 * tk
    nd = 2 * (seq // tq)
    return pl.pallas_call(
        functools.partial(_da_bias_kernel, tk=tk, tq=tq),
        grid=(DA_HEADS, nd),
        in_specs=[pl.BlockSpec(memory_space=pltpu.SMEM), pl.BlockSpec(memory_space=pltpu.SMEM)],
        out_specs=pl.BlockSpec((1, 1, tk, tq), lambda h, d: (h, d, 0, 0)),
        out_shape=jax.ShapeDtypeStruct((DA_HEADS, nd, tk, tq), F32),
        compiler_params=_cparams(("arbitrary", "arbitrary")),
        name="da_bias_tiles",
    )(thr, rel_bias)


def _dil_bias_kernel(thr_ref, tab_ref, o_ref):
    g = pl.program_id(0)
    hs = pl.program_id(1)
    n = DIL_KEYS
    dil = jnp.where(g == 0, DIL_PAIRS[0][1], jnp.where(g == 1, DIL_PAIRS[1][1], DIL_PAIRS[2][1]))
    row = lax.broadcasted_iota(jnp.int32, (n, 2 * n), 0)
    col = lax.broadcasted_iota(jnp.int32, (n, 2 * n), 1)
    step = row + n - col
    head = DA_HEADS + g * DIL_HEADS_PER_GROUP + hs
    val = _bias_lookup(step * dil, head, thr_ref, tab_ref)
    valid = jnp.where(step >= 0, jnp.where(step <= n, 1, 0), 0)
    o_ref[0, 0, 0] = jnp.where(valid > 0, val, NEG_INF)
    o_ref[0, 0, 1] = jnp.where(jnp.where(col >= n, valid, 0) > 0, val, NEG_INF)


def _dil_bias_tiles(thr, rel_bias):
    n = DIL_KEYS
    return pl.pallas_call(
        _dil_bias_kernel,
        grid=(DIL_GROUPS, DIL_HEADS_PER_GROUP),
        in_specs=[pl.BlockSpec(memory_space=pltpu.SMEM), pl.BlockSpec(memory_space=pltpu.SMEM)],
        out_specs=pl.BlockSpec((1, 1, 2, n, 2 * n), lambda g, h: (g, h, 0, 0, 0)),
        out_shape=jax.ShapeDtypeStruct((DIL_GROUPS, DIL_HEADS_PER_GROUP, 2, n, 2 * n), F32),
        compiler_params=_cparams(("arbitrary", "arbitrary")),
        name="dil_bias_tiles",
    )(thr, rel_bias)


def _proj_kernel(x_ref, w_ref, o_ref):
    o_ref[0] = _dot(x_ref[0], w_ref[...]).astype(o_ref.dtype)


def _proj(xb, w, tn):
    bsz, seq, d = xb.shape
    c = w.shape[1]
    tm = min(PROJ_TM, seq)
    return pl.pallas_call(
        _proj_kernel,
        grid=(bsz, seq // tm, c // tn),
        in_specs=[pl.BlockSpec((1, tm, d), lambda b, i, j: (b, i, 0)),
                  pl.BlockSpec((d, tn), lambda b, i, j: (0, j))],
        out_specs=pl.BlockSpec((1, tm, tn), lambda b, i, j: (b, i, j)),
        out_shape=jax.ShapeDtypeStruct((bsz, seq, c), BF16),
        compiler_params=_cparams(("parallel", "parallel", "arbitrary")),
        name="proj",
    )(xb, w)


def _proj_t_kernel(x_ref, wt_ref, o_ref, *, tile):
    res = _dot_nt(wt_ref[...], x_ref[0]).astype(o_ref.dtype)
    for c in range(o_ref.shape[1]):
        o_ref[0, c] = res[:, c * tile:(c + 1) * tile]


def _proj_t(xb, wt, tile):
    bsz, seq, d = xb.shape
    c = wt.shape[0]
    tm = min(PROJ_TM, seq)
    return pl.pallas_call(
        functools.partial(_proj_t_kernel, tile=tile),
        grid=(bsz, seq // tm),
        in_specs=[pl.BlockSpec((1, tm, d), lambda b, i: (b, i, 0)),
                  pl.BlockSpec((c, d), lambda b, i: (0, 0))],
        out_specs=pl.BlockSpec((1, tm // tile, c, tile), lambda b, i: (b, i, 0, 0)),
        out_shape=jax.ShapeDtypeStruct((bsz, seq // tile, c, tile), BF16),
        compiler_params=_cparams(("parallel", "arbitrary")),
        name="proj_t",
    )(xb, wt)


def _proj_dil_kernel(x_ref, w_ref, o_ref, res_ref, *, dil, rows):
    res = _dot(x_ref[0], w_ref[...])
    nblk = res.shape[1] // LANES
    for c in range(nblk):
        res_ref[c] = res[:, c * LANES:(c + 1) * LANES]
    for r in range(dil):
        for c in range(nblk):
            o_ref[0, r, :, c * LANES:(c + 1) * LANES] = (
                res_ref[c, pl.ds(r, rows, stride=dil), :].astype(o_ref.dtype))


def _proj_dil(xb, w, dil):
    bsz, seq, d = xb.shape
    c = w.shape[1]
    tm = min(PROJ_TM, seq)
    rows = tm // dil
    return pl.pallas_call(
        functools.partial(_proj_dil_kernel, dil=dil, rows=rows),
        grid=(bsz, seq // tm),
        in_specs=[pl.BlockSpec((1, tm, d), lambda b, i: (b, i, 0)),
                  pl.BlockSpec((d, c), lambda b, i: (0, 0))],
        out_specs=pl.BlockSpec((1, dil, rows, c), lambda b, i: (b, 0, i, 0)),
        out_shape=jax.ShapeDtypeStruct((bsz, dil, seq // dil, c), BF16),
        scratch_shapes=[pltpu.VMEM((c // LANES, tm, LANES), F32)],
        compiler_params=_cparams(("parallel", "arbitrary")),
        name="proj_dil",
    )(xb, w)


def _col_reduce(x, op):
    parts = [x[i:i + 8] for i in range(0, x.shape[0], 8)]
    while len(parts) > 1:
        parts = [op(parts[i], parts[i + 1]) for i in range(0, len(parts), 2)]
    return parts[0]


def _col_max(x):
    return jnp.max(_col_reduce(x, jnp.maximum), axis=0, keepdims=True)


def _col_sum(x):
    return jnp.sum(_col_reduce(x, jnp.add), axis=0, keepdims=True)


def _da_kernel(q_ref, k_ref, vt_ref, bias_ref, lam_ref, g_ref, o_ref, sa_ref, sb_ref, acc_ref, *, tk, tq, lam_init):
    qi = pl.program_id(2)
    nk = k_ref.shape[1] // tk
    lane = lax.broadcasted_iota(jnp.int32, (tq, LANES), 1)
    qs = q_ref[0] * jnp.asarray(DA_QK_DIM ** -0.5, BF16)
    zero = jnp.zeros_like(qs)
    q_maps = (jnp.where(lane < DA_QK_DIM, qs, zero), jnp.where(lane >= DA_QK_DIM, qs, zero))

    def put_scores(ki, dst):
        start = pl.multiple_of(ki * tk, tk)
        k = k_ref[0, pl.ds(start, tk), :]
        for mp in range(2):
            dst[mp] = _dot_nt(k, q_maps[mp])

    def softmax_pv(ki, src, stats):
        vt = vt_ref[0, ki]
        bias = bias_ref[0, 2 * qi - ki + 1]
        s = [src[mp] + bias for mp in range(2)]
        m_new = [jnp.maximum(stats[mp][0], _col_max(s[mp])) for mp in range(2)]
        out = []
        for mp in range(2):
            m, l = stats[mp]
            alpha = jnp.exp(m - m_new[mp])
            p = jnp.exp(s[mp] - m_new[mp])
            l = alpha * l + _col_sum(p)
            out.append((m_new[mp], l, alpha, _dot(vt, p.astype(BF16))))
        return out

    def pair(j, stats):
        ka = 2 * j
        put_scores(ka + 1, sb_ref)
        ra = softmax_pv(ka, sa_ref, stats)
        put_scores(jnp.minimum(ka + 2, nk - 1), sa_ref)
        rb = softmax_pv(ka + 1, sb_ref, tuple((r[0], r[1]) for r in ra))
        for mp in range(2):
            acc_ref[mp] = rb[mp][2] * (ra[mp][2] * acc_ref[mp] + ra[mp][3]) + rb[mp][3]
        return tuple((r[0], r[1]) for r in rb)

    put_scores(0, sa_ref)
    acc_ref[...] = jnp.zeros_like(acc_ref)
    init = tuple((jnp.full((1, tq), NEG_INF, F32), jnp.zeros((1, tq), F32)) for _ in range(2))
    (_, l1), (_, l2) = lax.fori_loop(0, qi + 1, pair, init)

    ll = lam_ref[...]
    lam = (jnp.exp(jnp.sum(ll[0:1] * ll[1:2], axis=-1, keepdims=True))
           - jnp.exp(jnp.sum(ll[2:3] * ll[3:4], axis=-1, keepdims=True)) + lam_init)
    o = acc_ref[0] * (1.0 / l1) - lam * (acc_ref[1] * (1.0 / l2))
    o = o * lax.rsqrt(jnp.mean(o * o, axis=0, keepdims=True) + RMS_EPS) * g_ref[...]
    o_ref[0] = (o * (1.0 - lam_init)).T.astype(o_ref.dtype)


def _da_attention(proj, vt, bias, lam_p, norm_g, lam_init):
    bsz, seq, _ = proj.shape
    tk, tq = bias.shape[-2:]
    nk = seq // tk
    kb = DA_COLS // LANES
    return pl.pallas_call(
        functools.partial(_da_kernel, tk=tk, tq=tq, lam_init=lam_init),
        grid=(DA_HEADS, bsz, seq // tq),
        in_specs=[pl.BlockSpec((1, tq, LANES), lambda h, b, i: (b, i, h)),
                  pl.BlockSpec((1, seq, LANES), lambda h, b, i: (b, 0, kb + h)),
                  pl.BlockSpec((1, nk, DA_V_DIM, tk), lambda h, b, i: (b, 0, h, 0)),
                  pl.BlockSpec((1, bias.shape[1], tk, tq), lambda h, b, i: (h, 0, 0, 0)),
                  pl.BlockSpec((4, DA_QK_DIM), lambda h, b, i: (0, 0)),
                  pl.BlockSpec((DA_V_DIM, 1), lambda h, b, i: (0, 0))],
        out_specs=pl.BlockSpec((1, tq, LANES), lambda h, b, i: (b, i, h)),
        out_shape=jax.ShapeDtypeStruct((bsz, seq, DA_COLS), BF16),
        scratch_shapes=[pltpu.VMEM((2, tk, tq), F32), pltpu.VMEM((2, tk, tq), F32),
                        pltpu.VMEM((2, DA_V_DIM, tq), F32)],
        compiler_params=_cparams(("arbitrary", "parallel", "arbitrary")),
        name="diff_attention",
    )(proj, proj, vt, bias, lam_p, norm_g)


def _sb_kernel(q_ref, k_ref, vt_ref, o_ref, za_ref, zb_ref, acc_ref, *, tk, tq):
    qi = pl.program_id(2)
    lane = lax.broadcasted_iota(jnp.int32, (tq, LANES), 1)
    qs = q_ref[0] * jnp.asarray(SB_DIM ** -0.5, BF16)
    zero = jnp.zeros_like(qs)
    q_heads = (jnp.where(lane < SB_DIM, qs, zero), jnp.where(lane >= SB_DIM, qs, zero))
    chan = lax.broadcasted_iota(jnp.int32, (LANES, tk), 0)
    head_chans = (chan < SB_DIM, chan >= SB_DIM)
    tri = jnp.where(lax.broadcasted_iota(jnp.int32, (tk, tk), 0) < lax.broadcasted_iota(jnp.int32, (tk, tk), 1),
                    1.0, 0.0).astype(BF16)
    tri2 = jnp.concatenate([tri, tri], axis=1)
    key = lax.broadcasted_iota(jnp.int32, (tk, tq), 0)
    query = lax.broadcasted_iota(jnp.int32, (tk, tq), 1)

    def put_scores(ki, dst):
        start = pl.multiple_of(ki * tk, tk)
        k = k_ref[0, pl.ds(start, tk), :]
        for h in range(2):
            dst[h] = _dot_nt(k, q_heads[h])

    def process(ki, src, cs, before):
        vt = vt_ref[0, ki]
        log_beta, log_keep, later = [], [], []
        for h in range(2):
            z = src[h]
            soft = jnp.log(1.0 + jnp.exp(-jnp.abs(z)))
            lb = jnp.minimum(z, 0.0) - soft
            lk = lb - z
            if before is not None:
                lk = jnp.where(before, lk, 0.0)
            hi = lk.astype(BF16)
            lo = (lk - hi.astype(F32)).astype(BF16)
            log_beta.append(lb)
            log_keep.append(lk)
            later.append(_dot(tri2, jnp.concatenate([hi, lo], axis=0)) + cs[h])
        new_cs = []
        part = None
        for h in range(2):
            a = jnp.exp(log_beta[h] + later[h])
            if before is not None:
                a = jnp.where(before, a, 0.0)
            new_cs.append(later[h][0:1, :] + log_keep[h][0:1, :])
            pv = _dot(jnp.where(head_chans[h], vt, jnp.zeros_like(vt)), a.astype(BF16))
            part = pv if part is None else part + pv
        return tuple(new_cs), part

    kd = 2 * qi + 1
    put_scores(kd, za_ref)
    put_scores(kd - 1, zb_ref)
    cs = (jnp.zeros((1, tq), F32), jnp.zeros((1, tq), F32))
    cs, part_a = process(kd, za_ref, cs, key + tk < query)
    put_scores(jnp.maximum(kd - 2, 0), za_ref)
    cs, part_b = process(kd - 1, zb_ref, cs, key < query)
    acc_ref[...] = part_a + part_b

    def pair(it, cs):
        ka = 2 * qi - 1 - 2 * it
        put_scores(ka - 1, zb_ref)
        cs, part_a = process(ka, za_ref, cs, None)
        put_scores(jnp.maximum(ka - 2, 0), za_ref)
        cs, part_b = process(ka - 1, zb_ref, cs, None)
        acc_ref[...] += part_a + part_b
        return cs

    lax.fori_loop(0, qi, pair, cs)
    o_ref[0] = acc_ref[...].T.astype(o_ref.dtype)


def _sb_attention(proj, vt):
    bsz, seq, _ = proj.shape
    tk = vt.shape[-1]
    tq = min(2 * tk, seq)
    nk = seq // tk
    qb = 2 * DA_COLS // LANES
    kb = qb + SB_COLS // LANES
    vb = DA_COLS // LANES
    return pl.pallas_call(
        functools.partial(_sb_kernel, tk=tk, tq=tq),
        grid=(bsz, SB_COLS // LANES, seq // tq),
        in_specs=[pl.BlockSpec((1, tq, LANES), lambda b, h, i: (b, i, qb + h)),
                  pl.BlockSpec((1, seq, LANES), lambda b, h, i: (b, 0, kb + h)),
                  pl.BlockSpec((1, nk, LANES, tk), lambda b, h, i: (b, 0, vb + h, 0))],
        out_specs=pl.BlockSpec((1, tq, LANES), lambda b, h, i: (b, i, h)),
        out_shape=jax.ShapeDtypeStruct((bsz, seq, SB_COLS), BF16),
        scratch_shapes=[pltpu.VMEM((2, tk, tq), F32), pltpu.VMEM((2, tk, tq), F32), pltpu.VMEM((LANES, tq), F32)],
        compiler_params=_cparams(("parallel", "parallel", "arbitrary")),
        name="stick_breaking_attention",
    )(proj, proj, vt)


def _dil_kernel(q_ref, kc_ref, kp_ref, vc_ref, vp_ref, bias_ref, o_ref, lse_ref, *, nblk):
    cc = pl.program_id(2)
    n = DIL_KEYS
    scale = DIL_DIM ** -0.5
    first = jnp.where(cc == 0, 1, 0)

    def block(qb, kp, kc, vp, vc, bias):
        sp = _dot_nt(qb, kp) * scale + bias[:, :n]
        sc = _dot_nt(qb, kc) * scale + bias[:, n:]
        m = jnp.maximum(jnp.max(sp, axis=-1, keepdims=True), jnp.max(sc, axis=-1, keepdims=True))
        pp = jnp.exp(sp - m)
        pc = jnp.exp(sc - m)
        l = jnp.sum(pp, axis=-1, keepdims=True) + jnp.sum(pc, axis=-1, keepdims=True)
        o = _dot(pp.astype(BF16), vp) + _dot(pc.astype(BF16), vc)
        return o * (1.0 / l), m + jnp.log(l)

    for hs in range(DIL_HEADS_PER_GROUP):
        cols = slice(hs * DIL_DIM, (hs + 1) * DIL_DIM)
        o, lse = block(q_ref[0, 0, 0:n, cols], kp_ref[0, 0, :, cols], kc_ref[0, 0, 0:n, cols],
                       vp_ref[0, 0, :, cols], vc_ref[0, 0, 0:n, cols], bias_ref[0, hs, first])
        o_ref[0, 0, 0:n, cols] = o
        lse_ref[0, 0, 0:n, cols] = jnp.broadcast_to(lse, (n, DIL_DIM))

        def body(j, _, cols=cols, hs=hs):
            cur = pl.ds(pl.multiple_of(j * n, n), n)
            prev = pl.ds(pl.multiple_of((j - 1) * n, n), n)
            o, lse = block(q_ref[0, 0, cur, cols], kc_ref[0, 0, prev, cols], kc_ref[0, 0, cur, cols],
                           vc_ref[0, 0, prev, cols], vc_ref[0, 0, cur, cols], bias_ref[0, hs, 0])
            o_ref[0, 0, cur, cols] = o
            lse_ref[0, 0, cur, cols] = jnp.broadcast_to(lse, (n, DIL_DIM))
            return 0

        lax.fori_loop(1, nblk, body, 0)


def _dil_attention(qkv, col_blk, bias, group):
    bsz, dil, length, _ = qkv.shape
    n = DIL_KEYS
    chunk = min(DIL_CHUNK, length)
    nblk = chunk // n
    w = DIL_COLS
    cur = lambda off: pl.BlockSpec((1, 1, chunk, w), lambda b, r, c: (b, r, c, col_blk + off))
    prev = lambda off: pl.BlockSpec((1, 1, n, w),
                                    lambda b, r, c: (b, r, jnp.maximum(c * nblk - 1, 0), col_blk + off))
    out_spec = pl.BlockSpec((1, 1, chunk, w), lambda b, r, c: (b, r, c, 0))
    shape = jax.ShapeDtypeStruct((bsz, dil, length, w), F32)
    return pl.pallas_call(
        functools.partial(_dil_kernel, nblk=nblk),
        grid=(bsz, dil, length // chunk),
        in_specs=[cur(0), cur(1), prev(1), cur(2), prev(2),
                  pl.BlockSpec((1, DIL_HEADS_PER_GROUP, 2, n, 2 * n), lambda b, r, c: (group, 0, 0, 0, 0))],
        out_specs=[out_spec, out_spec],
        out_shape=[shape, shape],
        compiler_params=_cparams(("parallel", "parallel", "arbitrary")),
        name="dilated_attention",
    )(qkv, qkv, qkv, qkv, qkv, bias)


def _merge_kernel(x_ref, xb_ref, oa_ref, ob_ref, o0_ref, l0_ref, o1_ref, l1_ref, o2_ref, l2_ref,
                  wa_ref, wb_ref, wc_ref, wg_ref, wo_ref, g_ref, b_ref,
                  y_ref, yb_ref, s1o, s1l, s2o, s2l, oc_ref, *, alpha, tm):
    for src, dst in ((o1_ref, s1o), (l1_ref, s1l), (o2_ref, s2o), (l2_ref, s2l)):
        dil = src.shape[1]
        rows = tm // dil
        for r in range(dil):
            for c in range(DIL_HEADS_PER_GROUP):
                dst[c, pl.ds(r, rows, stride=dil), :] = src[0, r, :, c * DIL_DIM:(c + 1) * DIL_DIM]
    for c in range(DIL_HEADS_PER_GROUP):
        cols = slice(c * DIL_DIM, (c + 1) * DIL_DIM)
        l0 = l0_ref[0, 0, :, cols]
        l1 = s1l[c]
        l2 = s2l[c]
        mx = jnp.maximum(jnp.maximum(l0, l1), l2)
        w0 = jnp.exp(l0 - mx)
        w1 = jnp.exp(l1 - mx)
        w2 = jnp.exp(l2 - mx)
        oc = (w0 * o0_ref[0, 0, :, cols] + w1 * s1o[c] + w2 * s2o[c]) * (1.0 / (w0 + w1 + w2))
        oc_ref[:, cols] = oc.astype(BF16)

    gates = _sigmoid(_dot(xb_ref[...], wg_ref[...]))
    d = D_MODEL
    merged = (gates[:, 0:d] * _dot(oa_ref[...], wa_ref[...])
              + gates[:, d:2 * d] * _dot(ob_ref[...], wb_ref[...])
              + gates[:, 2 * d:3 * d] * _dot(oc_ref[...], wc_ref[...]))
    mixed = _dot(merged.astype(BF16), wo_ref[...])
    y = _layer_norm(alpha * x_ref[...] + mixed, g_ref[...], b_ref[...])
    y_ref[...] = y
    yb_ref[...] = y.astype(BF16)


def _merge(x, xb, oa, ob, c0, c1, c2, wa, wb, wc, wg, wo, g, b, alpha):
    bsz, seq, d = x.shape
    tm = min(MERGE_TM, seq)
    row = lambda width: pl.BlockSpec((None, tm, width), lambda bb, i: (bb, i, 0))
    full = lambda arr: pl.BlockSpec(arr.shape, lambda bb, i: (0,) * arr.ndim)

    def strided(arr):
        dil = arr.shape[1]
        return pl.BlockSpec((1, dil, tm // dil, DIL_COLS), lambda bb, i: (bb, 0, i, 0))

    weights = (wa, wb, wc, wg, wo, g, b)
    return pl.pallas_call(
        functools.partial(_merge_kernel, alpha=alpha, tm=tm),
        grid=(bsz, seq // tm),
        in_specs=[row(d), row(d), row(DA_COLS), row(SB_COLS),
                  strided(c0[0]), strided(c0[1]), strided(c1[0]), strided(c1[1]),
                  strided(c2[0]), strided(c2[1])] + [full(a) for a in weights],
        out_specs=[row(d), row(d)],
        out_shape=[jax.ShapeDtypeStruct((bsz, seq, d), F32), jax.ShapeDtypeStruct((bsz, seq, d), BF16)],
        scratch_shapes=[pltpu.VMEM((DIL_HEADS_PER_GROUP, tm, DIL_DIM), F32) for _ in range(4)]
        + [pltpu.VMEM((tm, DIL_COLS), BF16)],
        compiler_params=_cparams(("parallel", "arbitrary")),
        name="merge_outproj_ln",
    )(x, xb, oa, ob, c0[0], c0[1], c1[0], c1[1], c2[0], c2[1], *weights)


def _mlp_kernel(x_ref, xb_ref, p_ref, wu_ref, wd_ref, wpg_ref, wp_ref, g_ref, b_ref,
                y_ref, yb_ref, acc_ref, *, alpha):
    f = pl.program_id(1)
    hid = jnp.maximum(_dot(xb_ref[...], wu_ref[...]), 0.0)
    part = _dot((hid * hid).astype(BF16), wd_ref[...])

    @pl.when(f == 0)
    def _():
        acc_ref[...] = part

    @pl.when(f > 0)
    def _():
        acc_ref[...] += part

    @pl.when(f == pl.num_programs(1) - 1)
    def _():
        ple = _sigmoid(_dot(xb_ref[...], wpg_ref[...])) * _dot(p_ref[...].astype(BF16), wp_ref[...])
        y = _layer_norm(alpha * x_ref[...] + acc_ref[...] + ple, g_ref[...], b_ref[...])
        y_ref[...] = y
        yb_ref[...] = y.astype(BF16)


def _mlp(x, xb, p, wu, wd, wpg, wp, g, b, alpha):
    n, d = x.shape
    tm = min(MLP_TM, n)
    tf = MLP_TF
    row = lambda width: pl.BlockSpec((tm, width), lambda i, f: (i, 0))
    full = lambda arr: pl.BlockSpec(arr.shape, lambda i, f: (0,) * arr.ndim)
    return pl.pallas_call(
        functools.partial(_mlp_kernel, alpha=alpha),
        grid=(n // tm, D_FF // tf),
        in_specs=[row(d), row(d), row(PLE_DIM),
                  pl.BlockSpec((d, tf), lambda i, f: (0, f)),
                  pl.BlockSpec((tf, d), lambda i, f: (f, 0)),
                  full(wpg), full(wp), full(g), full(b)],
        out_specs=[row(d), row(d)],
        out_shape=[jax.ShapeDtypeStruct((n, d), F32), jax.ShapeDtypeStruct((n, d), BF16)],
        scratch_shapes=[pltpu.VMEM((tm, d), F32)],
        compiler_params=_cparams(("parallel", "arbitrary")),
        name="mlp_ple_ln",
    )(x, xb, p, wu, wd, wpg, wp, g, b)


def kernel(x, p, w_in, da_lambda, da_norm, w_branch_da, w_branch_sb, w_branch_dil, w_out,
           ln1_g, ln1_b, w_up, w_down, w_ple_gate, w_ple, ln2_g, ln2_b, rel_bias):
    bsz, seq, d = x.shape
    depth = w_in.shape[0]
    n = bsz * seq
    alpha = (2 * depth) ** 0.25
    assert d == D_MODEL and all(wd // dl == DIL_KEYS for wd, dl in DIL_PAIRS)
    assert seq % (DIL_KEYS * DIL_PAIRS[-1][1]) == 0 and seq % min(ATT_TILE, seq) == 0

    qkv_a = 3 * DA_COLS
    qkv_b = 3 * SB_COLS
    c0 = qkv_a + qkv_b
    cw = DIL_GROUPS * DIL_COLS

    def dil_cols(g):
        return [w_in[:, :, c0 + t * cw + g * DIL_COLS: c0 + t * cw + (g + 1) * DIL_COLS] for t in range(3)]

    w_main = jnp.concatenate([w_in[:, :, :2 * DA_COLS], w_in[:, :, qkv_a:qkv_a + 2 * SB_COLS]] + dil_cols(0),
                             axis=-1).astype(BF16)
    w_vt = jnp.concatenate([w_in[:, :, 2 * DA_COLS:qkv_a], w_in[:, :, qkv_a + 2 * SB_COLS:c0]],
                           axis=-1).transpose(0, 2, 1).astype(BF16)
    main_c0 = 2 * DA_COLS + 2 * SB_COLS
    w_g1 = jnp.concatenate(dil_cols(1), axis=-1).astype(BF16)
    w_g2 = jnp.concatenate(dil_cols(2), axis=-1).astype(BF16)
    w_gate = w_in[:, :, c0 + 3 * cw:].astype(BF16)
    wa, wb, wc, wo = (w.astype(BF16) for w in (w_branch_da, w_branch_sb, w_branch_dil, w_out))
    wu, wd, wpg, wp = (w.astype(BF16) for w in (w_up, w_down, w_ple_gate, w_ple))

    dmax = max(seq, REL_MAX_DIST + 1)
    buckets = _rel_bucket(jnp.arange(dmax, dtype=jnp.int32))
    thr = jnp.sum(buckets[None, :] < jnp.arange(REL_BUCKETS, dtype=jnp.int32)[:, None], axis=1).astype(jnp.int32)
    tile = min(ATT_TILE, seq)
    da_bias = _da_bias_tiles(thr, rel_bias, seq, tile, min(DA_TQ, seq))
    dil_bias = _dil_bias_tiles(thr, rel_bias)

    xb = x.astype(BF16)
    for i in range(depth):
        lam_init = 0.8 - 0.6 * math.exp(-0.3 * i)
        proj = _proj(xb, w_main[i], w_main.shape[-1] // 2)
        vt = _proj_t(xb, w_vt[i], tile)
        proj_g1 = _proj_dil(xb, w_g1[i], DIL_PAIRS[1][1])
        proj_g2 = _proj_dil(xb, w_g2[i], DIL_PAIRS[2][1])
        oa = _da_attention(proj, vt, da_bias, da_lambda[i], da_norm[i].reshape(DA_V_DIM, 1), lam_init)
        ob = _sb_attention(proj, vt)
        cgrp0 = _dil_attention(proj.reshape(bsz, 1, seq, proj.shape[-1]), main_c0 // DIL_COLS, dil_bias, 0)
        cgrp1 = _dil_attention(proj_g1, 0, dil_bias, 1)
        cgrp2 = _dil_attention(proj_g2, 0, dil_bias, 2)
        x1, x1b = _merge(x, xb, oa, ob, cgrp0, cgrp1, cgrp2, wa[i], wb[i], wc[i], w_gate[i], wo[i],
                         ln1_g[i].reshape(1, d), ln1_b[i].reshape(1, d), alpha)
        x2, x2b = _mlp(x1.reshape(n, d), x1b.reshape(n, d), p[i].reshape(n, PLE_DIM), wu[i], wd[i], wpg[i], wp[i],
                       ln2_g[i].reshape(1, d), ln2_b[i].reshape(1, d), alpha)
        x = x2.reshape(bsz, seq, d)
        xb = x2b.reshape(bsz, seq, d)
    return x
```

```python
import functools
import math

import jax
import jax.numpy as jnp
from jax import lax
from jax.experimental import pallas as pl
from jax.experimental.pallas import tpu as pltpu

D_MODEL = 1024
DA_HEADS = 4
DA_QK_DIM = 64
DA_V_DIM = 2 * DA_QK_DIM
SB_HEADS = 8
SB_DIM = 64
DIL_PAIRS = ((128, 1), (512, 4), (2048, 16))
DIL_GROUPS = len(DIL_PAIRS)
DIL_HEADS_PER_GROUP = 4
DIL_DIM = 128
DIL_KEYS = 128
D_FF = 4 * D_MODEL
PLE_DIM = 256
REL_BUCKETS = 32
REL_MAX_DIST = 2048
BIAS_HEADS = DA_HEADS + DIL_GROUPS * DIL_HEADS_PER_GROUP
LN_EPS = 1e-5
RMS_EPS = 1e-5

DA_COLS = DA_HEADS * 2 * DA_QK_DIM
SB_COLS = SB_HEADS * SB_DIM
DIL_COLS = DIL_HEADS_PER_GROUP * DIL_DIM
GATE_COLS = 3 * D_MODEL

LANES = 128
VMEM_LIMIT = 48 * 1024 * 1024

ATT_TILE = 256
DA_TQ = 2 * ATT_TILE
PROJ_TM = 1024
MERGE_TM = 256
MLP_TM = 512
MLP_TF = 1024
DIL_CHUNK = 1024

F32 = jnp.float32
BF16 = jnp.bfloat16
NEG_INF = float("-inf")
SIGN_BIT = -2 ** 31
LOG2E = math.log2(math.e)


def _cparams(sem):
    return pltpu.CompilerParams(dimension_semantics=("arbitrary",) * len(sem), vmem_limit_bytes=VMEM_LIMIT)


def _dot(a, b):
    return jnp.dot(a, b, preferred_element_type=F32)


def _dot_nt(a, b):
    return lax.dot_general(a, b, (((1,), (1,)), ((), ())), preferred_element_type=F32)


def _sigmoid(v):
    return 1.0 / (1.0 + jnp.exp(-v))


def _layer_norm(y, g, b):
    mu = jnp.mean(y, axis=-1, keepdims=True)
    yc = y - mu
    var = jnp.mean(yc * yc, axis=-1, keepdims=True)
    return yc * lax.rsqrt(var + LN_EPS) * g + b


def _rel_bucket(dist):
    max_exact = REL_BUCKETS // 2
    d = jnp.maximum(dist, 0)
    log_ratio = jnp.log(jnp.maximum(d, 1).astype(F32) / max_exact) / math.log(REL_MAX_DIST / max_exact)
    large = max_exact + (log_ratio * (REL_BUCKETS - max_exact)).astype(jnp.int32)
    return jnp.where(d < max_exact, d, jnp.minimum(large, REL_BUCKETS - 1))


def _bias_lookup(d, head, thr_ref, tab_ref):
    val = jnp.full(d.shape, tab_ref[0, head], F32)
    for k in range(1, REL_BUCKETS):
        val = jnp.where(d >= thr_ref[k], tab_ref[k, head], val)
    return val


def _da_bias_kernel(thr_ref, tab_ref, o_ref, *, tk, tq):
    h = pl.program_id(0)
    delta = pl.program_id(1) - 1
    key = lax.broadcasted_iota(jnp.int32, (tk, tq), 0)
    query = lax.broadcasted_iota(jnp.int32, (tk, tq), 1)
    d = delta * tk + query - key
    val = _bias_lookup(d, h, thr_ref, tab_ref) * LOG2E
    o_ref[0, 0] = jnp.where(d >= 0, val, NEG_INF)


def _da_bias_tiles(thr, rel_bias, seq, tk, tq):
    assert tq == 2 * tk
    nd = 2 * (seq // tq)
    return pl.pallas_call(
        functools.partial(_da_bias_kernel, tk=tk, tq=tq),
        grid=(DA_HEADS, nd),
        in_specs=[pl.BlockSpec(memory_space=pltpu.SMEM), pl.BlockSpec(memory_space=pltpu.SMEM)],
        out_specs=pl.BlockSpec((1, 1, tk, tq), lambda h, d: (h, d, 0, 0)),
        out_shape=jax.ShapeDtypeStruct((DA_HEADS, nd, tk, tq), F32),
        compiler_params=_cparams(("arbitrary", "arbitrary")),
        name="da_bias_tiles",
    )(thr, rel_bias)


def _dil_bias_kernel(thr_ref, tab_ref, o_ref):
    g = pl.program_id(0)
    hs = pl.program_id(1)
    n = DIL_KEYS
    dil = jnp.where(g == 0, DIL_PAIRS[0][1], jnp.where(g == 1, DIL_PAIRS[1][1], DIL_PAIRS[2][1]))
    row = lax.broadcasted_iota(jnp.int32, (n, 2 * n), 0)
    col = lax.broadcasted_iota(jnp.int32, (n, 2 * n), 1)
    step = row + n - col
    head = DA_HEADS + g * DIL_HEADS_PER_GROUP + hs
    val = _bias_lookup(step * dil, head, thr_ref, tab_ref) * LOG2E
    valid = jnp.where(step >= 0, jnp.where(step <= n, 1, 0), 0)
    o_ref[0, 0, 0] = jnp.where(valid > 0, val, NEG_INF)
    o_ref[0, 0, 1] = jnp.where(jnp.where(col >= n, valid, 0) > 0, val, NEG_INF)


def _dil_bias_tiles(thr, rel_bias):
    n = DIL_KEYS
    return pl.pallas_call(
        _dil_bias_kernel,
        grid=(DIL_GROUPS, DIL_HEADS_PER_GROUP),
        in_specs=[pl.BlockSpec(memory_space=pltpu.SMEM), pl.BlockSpec(memory_space=pltpu.SMEM)],
        out_specs=pl.BlockSpec((1, 1, 2, n, 2 * n), lambda g, h: (g, h, 0, 0, 0)),
        out_shape=jax.ShapeDtypeStruct((DIL_GROUPS, DIL_HEADS_PER_GROUP, 2, n, 2 * n), F32),
        compiler_params=_cparams(("arbitrary", "arbitrary")),
        name="dil_bias_tiles",
    )(thr, rel_bias)


def _proj_kernel(x_ref, w_ref, cs_ref, o_ref):
    o_ref[0] = (_dot(x_ref[0], w_ref[...]) * cs_ref[...]).astype(o_ref.dtype)


def _proj(xb, w, col_scale, tn):
    bsz, seq, d = xb.shape
    c = w.shape[1]
    tm = min(PROJ_TM, seq)
    return pl.pallas_call(
        _proj_kernel,
        grid=(bsz, seq // tm, c // tn),
        in_specs=[pl.BlockSpec((1, tm, d), lambda b, i, j: (b, i, 0)),
                  pl.BlockSpec((d, tn), lambda b, i, j: (0, j)),
                  pl.BlockSpec((1, tn), lambda b, i, j: (0, j))],
        out_specs=pl.BlockSpec((1, tm, tn), lambda b, i, j: (b, i, j)),
        out_shape=jax.ShapeDtypeStruct((bsz, seq, c), BF16),
        compiler_params=_cparams(("parallel", "parallel", "arbitrary")),
        name="proj",
    )(xb, w, col_scale)


def _proj_t_kernel(x_ref, wt_ref, o_ref, *, tile):
    res = _dot_nt(wt_ref[...], x_ref[0]).astype(o_ref.dtype)
    for c in range(o_ref.shape[1]):
        o_ref[0, c] = res[:, c * tile:(c + 1) * tile]


def _proj_t(xb, wt, tile):
    bsz, seq, d = xb.shape
    c = wt.shape[0]
    tm = min(PROJ_TM, seq)
    return pl.pallas_call(
        functools.partial(_proj_t_kernel, tile=tile),
        grid=(bsz, seq // tm),
        in_specs=[pl.BlockSpec((1, tm, d), lambda b, i: (b, i, 0)),
                  pl.BlockSpec((c, d), lambda b, i: (0, 0))],
        out_specs=pl.BlockSpec((1, tm // tile, c, tile), lambda b, i: (b, i, 0, 0)),
        out_shape=jax.ShapeDtypeStruct((bsz, seq // tile, c, tile), BF16),
        compiler_params=_cparams(("parallel", "arbitrary")),
        name="proj_t",
    )(xb, wt)


def _proj_dil_kernel(x_ref, w_ref, cs_ref, o_ref, res_ref, *, dil, rows):
    res = _dot(x_ref[0], w_ref[...]) * cs_ref[...]
    nblk = res.shape[1] // LANES
    for c in range(nblk):
        res_ref[c] = res[:, c * LANES:(c + 1) * LANES]
    for r in range(dil):
        for c in range(nblk):
            o_ref[0, r, :, c * LANES:(c + 1) * LANES] = (
                res_ref[c, pl.ds(r, rows, stride=dil), :].astype(o_ref.dtype))


def _proj_dil(xb, w, col_scale, dil):
    bsz, seq, d = xb.shape
    c = w.shape[1]
    tm = min(PROJ_TM, seq)
    rows = tm // dil
    return pl.pallas_call(
        functools.partial(_proj_dil_kernel, dil=dil, rows=rows),
        grid=(bsz, seq // tm),
        in_specs=[pl.BlockSpec((1, tm, d), lambda b, i: (b, i, 0)),
                  pl.BlockSpec((d, c), lambda b, i: (0, 0)),
                  pl.BlockSpec((1, c), lambda b, i: (0, 0))],
        out_specs=pl.BlockSpec((1, dil, rows, c), lambda b, i: (b, 0, i, 0)),
        out_shape=jax.ShapeDtypeStruct((bsz, dil, seq // dil, c), BF16),
        scratch_shapes=[pltpu.VMEM((c // LANES, tm, LANES), F32)],
        compiler_params=_cparams(("parallel", "arbitrary")),
        name="proj_dil",
    )(xb, w, col_scale)


def _col_reduce(x, op):
    parts = [x[i:i + 8] for i in range(0, x.shape[0], 8)]
    while len(parts) > 1:
        parts = [op(parts[i], parts[i + 1]) for i in range(0, len(parts), 2)]
    return parts[0]


def _col_max(x):
    return jnp.max(_col_reduce(x, jnp.maximum), axis=0, keepdims=True)


def _col_sum(x):
    return jnp.sum(_col_reduce(x, jnp.add), axis=0, keepdims=True)


def _da_kernel(q_ref, k_ref, vt_ref, bias_ref, lam_ref, g_ref, o_ref, sa_ref, sb_ref, acc_ref, *, tk, tq, lam_init):
    qi = pl.program_id(2)
    nk = k_ref.shape[1] // tk
    lane = lax.broadcasted_iota(jnp.int32, (tq, LANES), 1)
    qs = q_ref[0]
    zero = jnp.zeros_like(qs)
    q_maps = (jnp.where(lane < DA_QK_DIM, qs, zero), jnp.where(lane >= DA_QK_DIM, qs, zero))

    def put_scores(ki, dst):
        start = pl.multiple_of(ki * tk, tk)
        k = k_ref[0, pl.ds(start, tk), :]
        for mp in range(2):
            dst[mp] = _dot_nt(k, q_maps[mp])

    def softmax_pv(ki, src, stats):
        vt = vt_ref[0, ki]
        bidx = 2 * qi - ki + 1
        out = []
        for mp in range(2):
            m, l = stats[mp]
            ms, ls, als, ps = [], [], [], []
            for c in range(tq // LANES):
                cols = slice(c * LANES, (c + 1) * LANES)
                s = src[mp, :, cols] + bias_ref[0, bidx, :, cols]
                m_new = jnp.maximum(m[:, cols], _col_max(s))
                alpha = jnp.exp2(m[:, cols] - m_new)
                p = jnp.exp2(s - m_new)
                ls.append(alpha * l[:, cols] + _col_sum(p))
                ms.append(m_new)
                als.append(alpha)
                ps.append(p.astype(BF16))
            alpha = jnp.concatenate(als, axis=1)
            acc_ref[mp] = alpha * acc_ref[mp] + _dot(vt, jnp.concatenate(ps, axis=1))
            out.append((jnp.concatenate(ms, axis=1), jnp.concatenate(ls, axis=1)))
        return tuple(out)

    def pair(j, stats):
        ka = 2 * j
        put_scores(ka + 1, sb_ref)
        stats = softmax_pv(ka, sa_ref, stats)
        put_scores(jnp.minimum(ka + 2, nk - 1), sa_ref)
        return softmax_pv(ka + 1, sb_ref, stats)

    put_scores(0, sa_ref)
    acc_ref[...] = jnp.zeros_like(acc_ref)
    init = tuple((jnp.full((1, tq), NEG_INF, F32), jnp.zeros((1, tq), F32)) for _ in range(2))
    (_, l1), (_, l2) = lax.fori_loop(0, qi + 1, pair, init)

    ll = lam_ref[...]
    lam = (jnp.exp(jnp.sum(ll[0:1] * ll[1:2], axis=-1, keepdims=True))
           - jnp.exp(jnp.sum(ll[2:3] * ll[3:4], axis=-1, keepdims=True)) + lam_init)
    o = acc_ref[0] * (1.0 / l1) - lam * (acc_ref[1] * (1.0 / l2))
    o = o * lax.rsqrt(jnp.mean(o * o, axis=0, keepdims=True) + RMS_EPS) * g_ref[...]
    o_ref[0] = (o * (1.0 - lam_init)).T.astype(o_ref.dtype)


def _da_attention(proj, vt, bias, lam_p, norm_g, lam_init):
    bsz, seq, _ = proj.shape
    tk, tq = bias.shape[-2:]
    nk = seq // tk
    kb = DA_COLS // LANES
    return pl.pallas_call(
        functools.partial(_da_kernel, tk=tk, tq=tq, lam_init=lam_init),
        grid=(DA_HEADS, bsz, seq // tq),
        in_specs=[pl.BlockSpec((1, tq, LANES), lambda h, b, i: (b, i, h)),
                  pl.BlockSpec((1, seq, LANES), lambda h, b, i: (b, 0, kb + h)),
                  pl.BlockSpec((1, nk, DA_V_DIM, tk), lambda h, b, i: (b, 0, h, 0)),
                  pl.BlockSpec((1, bias.shape[1], tk, tq), lambda h, b, i: (h, 0, 0, 0)),
                  pl.BlockSpec((4, DA_QK_DIM), lambda h, b, i: (0, 0)),
                  pl.BlockSpec((DA_V_DIM, 1), lambda h, b, i: (0, 0))],
        out_specs=pl.BlockSpec((1, tq, LANES), lambda h, b, i: (b, i, h)),
        out_shape=jax.ShapeDtypeStruct((bsz, seq, DA_COLS), BF16),
        scratch_shapes=[pltpu.VMEM((2, tk, tq), F32), pltpu.VMEM((2, tk, tq), F32),
                        pltpu.VMEM((2, DA_V_DIM, tq), F32)],
        compiler_params=_cparams(("arbitrary", "parallel", "arbitrary")),
        name="diff_attention",
    )(proj, proj, vt, bias, lam_p, norm_g)


def _sb_kernel(q_ref, k_ref, vt_ref, o_ref, za_ref, zb_ref, acc_ref, *, tk, tq):
    qi = pl.program_id(2)
    lane = lax.broadcasted_iota(jnp.int32, (tq, LANES), 1)
    qs = q_ref[0]
    zero = jnp.zeros_like(qs)
    q_heads = (jnp.where(lane < SB_DIM, qs, zero), jnp.where(lane >= SB_DIM, qs, zero))
    chan = lax.broadcasted_iota(jnp.int32, (LANES, tk), 0)
    head_chans = (chan < SB_DIM, chan >= SB_DIM)
    tri = jnp.where(lax.broadcasted_iota(jnp.int32, (tk, tk), 0) <= lax.broadcasted_iota(jnp.int32, (tk, tk), 1),
                    1.0, 0.0).astype(BF16)
    tri2 = jnp.concatenate([tri, tri], axis=1)
    key = lax.broadcasted_iota(jnp.int32, (tk, tq), 0)
    query = lax.broadcasted_iota(jnp.int32, (tk, tq), 1)

    def put_scores(ki, dst):
        start = pl.multiple_of(ki * tk, tk)
        k = k_ref[0, pl.ds(start, tk), :]
        for h in range(2):
            dst[h] = _dot_nt(k, q_heads[h])

    def process(ki, src, cs, before):
        vt = vt_ref[0, ki]
        mass = []
        for h in range(2):
            z = src[h]
            neg_abs = pltpu.bitcast(pltpu.bitcast(z, jnp.int32) | SIGN_BIT, F32)
            drop = jnp.maximum(z, 0.0) + jnp.log2(1.0 + jnp.exp2(neg_abs))
            if before is not None:
                drop = jnp.where(before, drop, 0.0)
            hi = drop.astype(BF16)
            lo = (drop - hi.astype(F32)).astype(BF16)
            mass.append(_dot(tri2, jnp.concatenate([hi, lo], axis=0)) + cs[h])
        new_cs = []
        part = None
        for h in range(2):
            a = jnp.exp2(src[h] - mass[h])
            if before is not None:
                a = jnp.where(before, a, 0.0)
            new_cs.append(mass[h][0:1, :])
            pv = _dot(jnp.where(head_chans[h], vt, jnp.zeros_like(vt)), a.astype(BF16))
            part = pv if part is None else part + pv
        return tuple(new_cs), part

    kd = 2 * qi + 1
    put_scores(kd, za_ref)
    put_scores(kd - 1, zb_ref)
    cs = (jnp.zeros((1, tq), F32), jnp.zeros((1, tq), F32))
    cs, part_a = process(kd, za_ref, cs, key + tk < query)
    put_scores(jnp.maximum(kd - 2, 0), za_ref)
    cs, part_b = process(kd - 1, zb_ref, cs, key < query)
    acc_ref[...] = part_a + part_b

    def pair(it, cs):
        ka = 2 * qi - 1 - 2 * it
        put_scores(ka - 1, zb_ref)
        cs, part_a = process(ka, za_ref, cs, None)
        put_scores(jnp.maximum(ka - 2, 0), za_ref)
        cs, part_b = process(ka - 1, zb_ref, cs, None)
        acc_ref[...] += part_a + part_b
        return cs

    lax.fori_loop(0, qi, pair, cs)
    o_ref[0] = acc_ref[...].T.astype(o_ref.dtype)


def _sb_attention(proj, vt):
    bsz, seq, _ = proj.shape
    tk = vt.shape[-1]
    tq = min(2 * tk, seq)
    nk = seq // tk
    qb = 2 * DA_COLS // LANES
    kb = qb + SB_COLS // LANES
    vb = DA_COLS // LANES
    return pl.pallas_call(
        functools.partial(_sb_kernel, tk=tk, tq=tq),
        grid=(bsz, SB_COLS // LANES, seq // tq),
        in_specs=[pl.BlockSpec((1, tq, LANES), lambda b, h, i: (b, i, qb + h)),
                  pl.BlockSpec((1, seq, LANES), lambda b, h, i: (b, 0, kb + h)),
                  pl.BlockSpec((1, nk, LANES, tk), lambda b, h, i: (b, 0, vb + h, 0))],
        out_specs=pl.BlockSpec((1, tq, LANES), lambda b, h, i: (b, i, h)),
        out_shape=jax.ShapeDtypeStruct((bsz, seq, SB_COLS), BF16),
        scratch_shapes=[pltpu.VMEM((2, tk, tq), F32), pltpu.VMEM((2, tk, tq), F32), pltpu.VMEM((LANES, tq), F32)],
        compiler_params=_cparams(("parallel", "parallel", "arbitrary")),
        name="stick_breaking_attention",
    )(proj, proj, vt)


def _dil_kernel(q_ref, kc_ref, kp_ref, vc_ref, vp_ref, bias_ref, o_ref, lse_ref, *, nblk):
    cc = pl.program_id(2)
    n = DIL_KEYS
    first = jnp.where(cc == 0, 1, 0)
    heads = range(DIL_HEADS_PER_GROUP)
    hcols = [slice(hs * DIL_DIM, (hs + 1) * DIL_DIM) for hs in heads]

    def blocks(q_rows, prev_ref, prev_rows, cur_rows, variant):
        sp, sc = [], []
        for hs in heads:
            qb = q_ref[0, 0, q_rows, hcols[hs]]
            bias = bias_ref[0, hs, variant]
            sp.append(_dot_nt(qb, prev_ref[0, 0, prev_rows, hcols[hs]]) + bias[:, :n])
            sc.append(_dot_nt(qb, kc_ref[0, 0, cur_rows, hcols[hs]]) + bias[:, n:])
        m = [jnp.maximum(jnp.max(sp[hs], axis=-1, keepdims=True), jnp.max(sc[hs], axis=-1, keepdims=True))
             for hs in heads]
        outs = []
        for hs in heads:
            pp = jnp.exp2(sp[hs] - m[hs])
            pc = jnp.exp2(sc[hs] - m[hs])
            l = jnp.sum(pp, axis=-1, keepdims=True) + jnp.sum(pc, axis=-1, keepdims=True)
            vprev = (vp_ref if prev_ref is kp_ref else vc_ref)[0, 0, prev_rows, hcols[hs]]
            o = _dot(pp.astype(BF16), vprev) + _dot(pc.astype(BF16), vc_ref[0, 0, cur_rows, hcols[hs]])
            outs.append((o, l))
        for hs in heads:
            o, l = outs[hs]
            o_ref[0, 0, q_rows, hcols[hs]] = o * (1.0 / l)
            lse_ref[0, 0, q_rows, hcols[hs]] = jnp.broadcast_to(m[hs] + jnp.log2(l), (n, DIL_DIM))

    blocks(slice(0, n), kp_ref, slice(None), slice(0, n), first)

    def body(j, _):
        cur = pl.ds(pl.multiple_of(j * n, n), n)
        prev = pl.ds(pl.multiple_of((j - 1) * n, n), n)
        blocks(cur, kc_ref, prev, cur, 0)
        return 0

    lax.fori_loop(1, nblk, body, 0)


def _dil_attention(qkv, col_blk, bias, group):
    bsz, dil, length, _ = qkv.shape
    n = DIL_KEYS
    chunk = min(DIL_CHUNK, length)
    nblk = chunk // n
    w = DIL_COLS
    cur = lambda off: pl.BlockSpec((1, 1, chunk, w), lambda b, r, c: (b, r, c, col_blk + off))
    prev = lambda off: pl.BlockSpec((1, 1, n, w),
                                    lambda b, r, c: (b, r, jnp.maximum(c * nblk - 1, 0), col_blk + off))
    out_spec = pl.BlockSpec((1, 1, chunk, w), lambda b, r, c: (b, r, c, 0))
    shape = jax.ShapeDtypeStruct((bsz, dil, length, w), F32)
    return pl.pallas_call(
        functools.partial(_dil_kernel, nblk=nblk),
        grid=(bsz, dil, length // chunk),
        in_specs=[cur(0), cur(1), prev(1), cur(2), prev(2),
                  pl.BlockSpec((1, DIL_HEADS_PER_GROUP, 2, n, 2 * n), lambda b, r, c: (group, 0, 0, 0, 0))],
        out_specs=[out_spec, out_spec],
        out_shape=[shape, shape],
        compiler_params=_cparams(("parallel", "parallel", "arbitrary")),
        name="dilated_attention",
    )(qkv, qkv, qkv, qkv, qkv, bias)


def _merge_kernel(x_ref, xb_ref, oa_ref, ob_ref, o0_ref, l0_ref, o1_ref, l1_ref, o2_ref, l2_ref,
                  wa_ref, wb_ref, wc_ref, wg_ref, wo_ref, g_ref, b_ref,
                  y_ref, yb_ref, s1o, s1l, s2o, s2l, oc_ref, *, alpha, tm):
    for src, dst in ((o1_ref, s1o), (l1_ref, s1l), (o2_ref, s2o), (l2_ref, s2l)):
        dil = src.shape[1]
        rows = tm // dil
        for r in range(dil):
            for c in range(DIL_HEADS_PER_GROUP):
                dst[c, pl.ds(r, rows, stride=dil), :] = src[0, r, :, c * DIL_DIM:(c + 1) * DIL_DIM]
    for c in range(DIL_HEADS_PER_GROUP):
        cols = slice(c * DIL_DIM, (c + 1) * DIL_DIM)
        l0 = l0_ref[0, 0, :, cols]
        l1 = s1l[c]
        l2 = s2l[c]
        mx = jnp.maximum(jnp.maximum(l0, l1), l2)
        w0 = jnp.exp2(l0 - mx)
        w1 = jnp.exp2(l1 - mx)
        w2 = jnp.exp2(l2 - mx)
        oc = (w0 * o0_ref[0, 0, :, cols] + w1 * s1o[c] + w2 * s2o[c]) * (1.0 / (w0 + w1 + w2))
        oc_ref[:, cols] = oc.astype(BF16)

    gates = _sigmoid(_dot(xb_ref[...], wg_ref[...]))
    d = D_MODEL
    merged = (gates[:, 0:d] * _dot(oa_ref[...], wa_ref[...])
              + gates[:, d:2 * d] * _dot(ob_ref[...], wb_ref[...])
              + gates[:, 2 * d:3 * d] * _dot(oc_ref[...], wc_ref[...]))
    mixed = _dot(merged.astype(BF16), wo_ref[...])
    y = _layer_norm(alpha * x_ref[...] + mixed, g_ref[...], b_ref[...])
    y_ref[...] = y
    yb_ref[...] = y.astype(BF16)


def _merge(x, xb, oa, ob, c0, c1, c2, wa, wb, wc, wg, wo, g, b, alpha):
    bsz, seq, d = x.shape
    tm = min(MERGE_TM, seq)
    row = lambda width: pl.BlockSpec((None, tm, width), lambda bb, i: (bb, i, 0))
    full = lambda arr: pl.BlockSpec(arr.shape, lambda bb, i: (0,) * arr.ndim)

    def strided(arr):
        dil = arr.shape[1]
        return pl.BlockSpec((1, dil, tm // dil, DIL_COLS), lambda bb, i: (bb, 0, i, 0))

    weights = (wa, wb, wc, wg, wo, g, b)
    return pl.pallas_call(
        functools.partial(_merge_kernel, alpha=alpha, tm=tm),
        grid=(bsz, seq // tm),
        in_specs=[row(d), row(d), row(DA_COLS), row(SB_COLS),
                  strided(c0[0]), strided(c0[1]), strided(c1[0]), strided(c1[1]),
                  strided(c2[0]), strided(c2[1])] + [full(a) for a in weights],
        out_specs=[row(d), row(d)],
        out_shape=[jax.ShapeDtypeStruct((bsz, seq, d), F32), jax.ShapeDtypeStruct((bsz, seq, d), BF16)],
        scratch_shapes=[pltpu.VMEM((DIL_HEADS_PER_GROUP, tm, DIL_DIM), F32) for _ in range(4)]
        + [pltpu.VMEM((tm, DIL_COLS), BF16)],
        compiler_params=_cparams(("parallel", "arbitrary")),
        name="merge_outproj_ln",
    )(x, xb, oa, ob, c0[0], c0[1], c1[0], c1[1], c2[0], c2[1], *weights)


def _mlp_kernel(x_ref, xb_ref, p_ref, wu_ref, wd_ref, wpg_ref, wp_ref, g_ref, b_ref,
                y_ref, yb_ref, acc_ref, *, alpha):
    f = pl.program_id(1)
    hid = jnp.maximum(_dot(xb_ref[...], wu_ref[...]), 0.0)
    part = _dot((hid * hid).astype(BF16), wd_ref[...])

    @pl.when(f == 0)
    def _():
        acc_ref[...] = part

    @pl.when(f > 0)
    def _():
        acc_ref[...] += part

    @pl.when(f == pl.num_programs(1) - 1)
    def _():
        ple = _sigmoid(_dot(xb_ref[...], wpg_ref[...])) * _dot(p_ref[...].astype(BF16), wp_ref[...])
        y = _layer_norm(alpha * x_ref[...] + acc_ref[...] + ple, g_ref[...], b_ref[...])
        y_ref[...] = y
        yb_ref[...] = y.astype(BF16)


def _mlp(x, xb, p, wu, wd, wpg, wp, g, b, alpha):
    n, d = x.shape
    tm = min(MLP_TM, n)
    tf = MLP_TF
    row = lambda width: pl.BlockSpec((tm, width), lambda i, f: (i, 0))
    full = lambda arr: pl.BlockSpec(arr.shape, lambda i, f: (0,) * arr.ndim)
    return pl.pallas_call(
        functools.partial(_mlp_kernel, alpha=alpha),
        grid=(n // tm, D_FF // tf),
        in_specs=[row(d), row(d), row(PLE_DIM),
                  pl.BlockSpec((d, tf), lambda i, f: (0, f)),
                  pl.BlockSpec((tf, d), lambda i, f: (f, 0)),
                  full(wpg), full(wp), full(g), full(b)],
        out_specs=[row(d), row(d)],
        out_shape=[jax.ShapeDtypeStruct((n, d), F32), jax.ShapeDtypeStruct((n, d), BF16)],
        scratch_shapes=[pltpu.VMEM((tm, d), F32)],
        compiler_params=_cparams(("parallel", "arbitrary")),
        name="mlp_ple_ln",
    )(x, xb, p, wu, wd, wpg, wp, g, b)


def kernel(x, p, w_in, da_lambda, da_norm, w_branch_da, w_branch_sb, w_branch_dil, w_out,
           ln1_g, ln1_b, w_up, w_down, w_ple_gate, w_ple, ln2_g, ln2_b, rel_bias):
    bsz, seq, d = x.shape
    depth = w_in.shape[0]
    n = bsz * seq
    alpha = (2 * depth) ** 0.25
    assert d == D_MODEL and all(wd // dl == DIL_KEYS for wd, dl in DIL_PAIRS)
    assert seq % (DIL_KEYS * DIL_PAIRS[-1][1]) == 0 and seq % min(ATT_TILE, seq) == 0

    qkv_a = 3 * DA_COLS
    qkv_b = 3 * SB_COLS
    c0 = qkv_a + qkv_b
    cw = DIL_GROUPS * DIL_COLS

    def dil_cols(g):
        return [w_in[:, :, c0 + t * cw + g * DIL_COLS: c0 + t * cw + (g + 1) * DIL_COLS] for t in range(3)]

    w_main = jnp.concatenate([w_in[:, :, :2 * DA_COLS], w_in[:, :, qkv_a:qkv_a + 2 * SB_COLS]] + dil_cols(0),
                             axis=-1).astype(BF16)
    w_vt = jnp.concatenate([w_in[:, :, 2 * DA_COLS:qkv_a], w_in[:, :, qkv_a + 2 * SB_COLS:c0]],
                           axis=-1).transpose(0, 2, 1).astype(BF16)
    main_c0 = 2 * DA_COLS + 2 * SB_COLS
    ones = lambda width: jnp.ones((width,), F32)
    qscale = lambda width, dh: jnp.full((width,), LOG2E * dh ** -0.5, F32)
    cs_main = jnp.concatenate([qscale(DA_COLS, DA_QK_DIM), ones(DA_COLS), qscale(SB_COLS, SB_DIM), ones(SB_COLS),
                               qscale(DIL_COLS, DIL_DIM), ones(2 * DIL_COLS)]).reshape(1, -1)
    cs_dil = jnp.concatenate([qscale(DIL_COLS, DIL_DIM), ones(2 * DIL_COLS)]).reshape(1, -1)
    w_g1 = jnp.concatenate(dil_cols(1), axis=-1).astype(BF16)
    w_g2 = jnp.concatenate(dil_cols(2), axis=-1).astype(BF16)
    w_gate = w_in[:, :, c0 + 3 * cw:].astype(BF16)
    wa, wb, wc, wo = (w.astype(BF16) for w in (w_branch_da, w_branch_sb, w_branch_dil, w_out))
    wu, wd, wpg, wp = (w.astype(BF16) for w in (w_up, w_down, w_ple_gate, w_ple))

    dmax = max(seq, REL_MAX_DIST + 1)
    buckets = _rel_bucket(jnp.arange(dmax, dtype=jnp.int32))
    thr = jnp.sum(buckets[None, :] < jnp.arange(REL_BUCKETS, dtype=jnp.int32)[:, None], axis=1).astype(jnp.int32)
    tile = min(ATT_TILE, seq)
    da_bias = _da_bias_tiles(thr, rel_bias, seq, tile, min(DA_TQ, seq))
    dil_bias = _dil_bias_tiles(thr, rel_bias)

    xb = x.astype(BF16)
    for i in range(depth):
        lam_init = 0.8 - 0.6 * math.exp(-0.3 * i)
        proj = _proj(xb, w_main[i], cs_main, w_main.shape[-1] // 2)
        vt = _proj_t(xb, w_vt[i], tile)
        proj_g1 = _proj_dil(xb, w_g1[i], cs_dil, DIL_PAIRS[1][1])
        proj_g2 = _proj_dil(xb, w_g2[i], cs_dil, DIL_PAIRS[2][1])
        oa = _da_attention(proj, vt, da_bias, da_lambda[i], da_norm[i].reshape(DA_V_DIM, 1), lam_init)
        ob = _sb_attention(proj, vt)
        cgrp0 = _dil_attention(proj.reshape(bsz, 1, seq, proj.shape[-1]), main_c0 // DIL_COLS, dil_bias, 0)
        cgrp1 = _dil_attention(proj_g1, 0, dil_bias, 1)
        cgrp2 = _dil_attention(proj_g2, 0, dil_bias, 2)
        x1, x1b = _merge(x, xb, oa, ob, cgrp0, cgrp1, cgrp2, wa[i], wb[i], wc[i], w_gate[i], wo[i],
                         ln1_g[i].reshape(1, d), ln1_b[i].reshape(1, d), alpha)
        x2, x2b = _mlp(x1.reshape(n, d), x1b.reshape(n, d), p[i].reshape(n, PLE_DIM), wu[i], wd[i], wpg[i], wp[i],
                       ln2_g[i].reshape(1, d), ln2_b[i].reshape(1, d), alpha)
        x = x2.reshape(bsz, seq, d)
        xb = x2b.reshape(bsz, seq, d)
    return x
```

```python
import functools
import math

import jax
import jax.numpy as jnp
from jax import lax
from jax.experimental import pallas as pl
from jax.experimental.pallas import tpu as pltpu

D_MODEL = 1024
DA_HEADS = 4
DA_QK_DIM = 64
DA_V_DIM = 2 * DA_QK_DIM
SB_HEADS = 8
SB_DIM = 64
DIL_PAIRS = ((128, 1), (512, 4), (2048, 16))
DIL_GROUPS = len(DIL_PAIRS)
DIL_HEADS_PER_GROUP = 4
DIL_DIM = 128
DIL_KEYS = 128
D_FF = 4 * D_MODEL
PLE_DIM = 256
REL_BUCKETS = 32
REL_MAX_DIST = 2048
BIAS_HEADS = DA_HEADS + DIL_GROUPS * DIL_HEADS_PER_GROUP
LN_EPS = 1e-5
RMS_EPS = 1e-5

DA_COLS = DA_HEADS * 2 * DA_QK_DIM
SB_COLS = SB_HEADS * SB_DIM
DIL_COLS = DIL_HEADS_PER_GROUP * DIL_DIM
GATE_COLS = 3 * D_MODEL

LANES = 128
VMEM_LIMIT = 48 * 1024 * 1024

ATT_TILE = 256
DA_TQ = 2 * ATT_TILE
PROJ_TM = 1024
MERGE_TM = 256
MLP_TM = 512
MLP_TF = 1024
DIL_CHUNK = 1024

F32 = jnp.float32
BF16 = jnp.bfloat16
NEG_INF = float("-inf")
SIGN_BIT = -2 ** 31
LOG2E = math.log2(math.e)


def _cparams(sem):
    return pltpu.CompilerParams(dimension_semantics=("arbitrary",) * len(sem), vmem_limit_bytes=VMEM_LIMIT)


def _dot(a, b):
    return jnp.dot(a, b, preferred_element_type=F32)


def _dot_nt(a, b):
    return lax.dot_general(a, b, (((1,), (1,)), ((), ())), preferred_element_type=F32)


def _sigmoid(v):
    return 1.0 / (1.0 + jnp.exp(-v))


def _layer_norm(y, g, b):
    mu = jnp.mean(y, axis=-1, keepdims=True)
    yc = y - mu
    var = jnp.mean(yc * yc, axis=-1, keepdims=True)
    return yc * lax.rsqrt(var + LN_EPS) * g + b


def _rel_bucket(dist):
    max_exact = REL_BUCKETS // 2
    d = jnp.maximum(dist, 0)
    log_ratio = jnp.log(jnp.maximum(d, 1).astype(F32) / max_exact) / math.log(REL_MAX_DIST / max_exact)
    large = max_exact + (log_ratio * (REL_BUCKETS - max_exact)).astype(jnp.int32)
    return jnp.where(d < max_exact, d, jnp.minimum(large, REL_BUCKETS - 1))


def _bias_lookup(d, head, thr_ref, tab_ref):
    val = jnp.full(d.shape, tab_ref[0, head], F32)
    for k in range(1, REL_BUCKETS):
        val = jnp.where(d >= thr_ref[k], tab_ref[k, head], val)
    return val


def _da_bias_kernel(thr_ref, tab_ref, o_ref, *, tk, tq):
    h = pl.program_id(0)
    delta = pl.program_id(1) - 1
    key = lax.broadcasted_iota(jnp.int32, (tk, tq), 0)
    query = lax.broadcasted_iota(jnp.int32, (tk, tq), 1)
    d = delta * tk + query - key
    val = _bias_lookup(d, h, thr_ref, tab_ref) * LOG2E
    o_ref[0, 0] = jnp.where(d >= 0, val, NEG_INF)


def _da_bias_tiles(thr, rel_bias, seq, tk, tq):
    assert tq == 2 * tk
    nd = 2 * (seq // tq)
    return pl.pallas_call(
        functools.partial(_da_bias_kernel, tk=tk, tq=tq),
        grid=(DA_HEADS, nd),
        in_specs=[pl.BlockSpec(memory_space=pltpu.SMEM), pl.BlockSpec(memory_space=pltpu.SMEM)],
        out_specs=pl.BlockSpec((1, 1, tk, tq), lambda h, d: (h, d, 0, 0)),
        out_shape=jax.ShapeDtypeStruct((DA_HEADS, nd, tk, tq), F32),
        compiler_params=_cparams(("arbitrary", "arbitrary")),
        name="da_bias_tiles",
    )(thr, rel_bias)


def _dil_bias_kernel(thr_ref, tab_ref, o_ref):
    g = pl.program_id(0)
    hs = pl.program_id(1)
    n = DIL_KEYS
    dil = jnp.where(g == 0, DIL_PAIRS[0][1], jnp.where(g == 1, DIL_PAIRS[1][1], DIL_PAIRS[2][1]))
    row = lax.broadcasted_iota(jnp.int32, (n, 2 * n), 0)
    col = lax.broadcasted_iota(jnp.int32, (n, 2 * n), 1)
    step = row + n - col
    head = DA_HEADS + g * DIL_HEADS_PER_GROUP + hs
    val = _bias_lookup(step * dil, head, thr_ref, tab_ref) * LOG2E
    valid = jnp.where(step >= 0, jnp.where(step <= n, 1, 0), 0)
    o_ref[0, 0, 0] = jnp.where(valid > 0, val, NEG_INF)
    o_ref[0, 0, 1] = jnp.where(jnp.where(col >= n, valid, 0) > 0, val, NEG_INF)


def _dil_bias_tiles(thr, rel_bias):
    n = DIL_KEYS
    return pl.pallas_call(
        _dil_bias_kernel,
        grid=(DIL_GROUPS, DIL_HEADS_PER_GROUP),
        in_specs=[pl.BlockSpec(memory_space=pltpu.SMEM), pl.BlockSpec(memory_space=pltpu.SMEM)],
        out_specs=pl.BlockSpec((1, 1, 2, n, 2 * n), lambda g, h: (g, h, 0, 0, 0)),
        out_shape=jax.ShapeDtypeStruct((DIL_GROUPS, DIL_HEADS_PER_GROUP, 2, n, 2 * n), F32),
        compiler_params=_cparams(("arbitrary", "arbitrary")),
        name="dil_bias_tiles",
    )(thr, rel_bias)


def _proj_kernel(x_ref, w_ref, cs_ref, o_ref):
    o_ref[0] = (_dot(x_ref[0], w_ref[...]) * cs_ref[...]).astype(o_ref.dtype)


def _proj(xb, w, col_scale, tn):
    bsz, seq, d = xb.shape
    c = w.shape[1]
    tm = min(PROJ_TM, seq)
    return pl.pallas_call(
        _proj_kernel,
        grid=(bsz, seq // tm, c // tn),
        in_specs=[pl.BlockSpec((1, tm, d), lambda b, i, j: (b, i, 0)),
                  pl.BlockSpec((d, tn), lambda b, i, j: (0, j)),
                  pl.BlockSpec((1, tn), lambda b, i, j: (0, j))],
        out_specs=pl.BlockSpec((1, tm, tn), lambda b, i, j: (b, i, j)),
        out_shape=jax.ShapeDtypeStruct((bsz, seq, c), BF16),
        compiler_params=_cparams(("parallel", "parallel", "arbitrary")),
        name="proj",
    )(xb, w, col_scale)


def _proj_t_kernel(x_ref, wt_ref, o_ref, *, tile):
    res = _dot_nt(wt_ref[...], x_ref[0]).astype(o_ref.dtype)
    for c in range(o_ref.shape[1]):
        o_ref[0, c] = res[:, c * tile:(c + 1) * tile]


def _proj_t(xb, wt, tile):
    bsz, seq, d = xb.shape
    c = wt.shape[0]
    tm = min(PROJ_TM, seq)
    return pl.pallas_call(
        functools.partial(_proj_t_kernel, tile=tile),
        grid=(bsz, seq // tm),
        in_specs=[pl.BlockSpec((1, tm, d), lambda b, i: (b, i, 0)),
                  pl.BlockSpec((c, d), lambda b, i: (0, 0))],
        out_specs=pl.BlockSpec((1, tm // tile, c, tile), lambda b, i: (b, i, 0, 0)),
        out_shape=jax.ShapeDtypeStruct((bsz, seq // tile, c, tile), BF16),
        compiler_params=_cparams(("parallel", "arbitrary")),
        name="proj_t",
    )(xb, wt)


def _proj_dil_kernel(x_ref, w_ref, cs_ref, o_ref, res_ref, *, dil, rows):
    res = _dot(x_ref[0], w_ref[...]) * cs_ref[...]
    nblk = res.shape[1] // LANES
    for c in range(nblk):
        res_ref[c] = res[:, c * LANES:(c + 1) * LANES]
    for r in range(dil):
        for c in range(nblk):
            o_ref[0, r, :, c * LANES:(c + 1) * LANES] = (
                res_ref[c, pl.ds(r, rows, stride=dil), :].astype(o_ref.dtype))


def _proj_dil(xb, w, col_scale, dil):
    bsz, seq, d = xb.shape
    c = w.shape[1]
    tm = min(PROJ_TM, seq)
    rows = tm // dil
    return pl.pallas_call(
        functools.partial(_proj_dil_kernel, dil=dil, rows=rows),
        grid=(bsz, seq // tm),
        in_specs=[pl.BlockSpec((1, tm, d), lambda b, i: (b, i, 0)),
                  pl.BlockSpec((d, c), lambda b, i: (0, 0)),
                  pl.BlockSpec((1, c), lambda b, i: (0, 0))],
        out_specs=pl.BlockSpec((1, dil, rows, c), lambda b, i: (b, 0, i, 0)),
        out_shape=jax.ShapeDtypeStruct((bsz, dil, seq // dil, c), BF16),
        scratch_shapes=[pltpu.VMEM((c // LANES, tm, LANES), F32)],
        compiler_params=_cparams(("parallel", "arbitrary")),
        name="proj_dil",
    )(xb, w, col_scale)


def _col_reduce(x, op):
    parts = [x[i:i + 8] for i in range(0, x.shape[0], 8)]
    while len(parts) > 1:
        parts = [op(parts[i], parts[i + 1]) for i in range(0, len(parts), 2)]
    return parts[0]


def _col_max(x):
    return jnp.max(_col_reduce(x, jnp.maximum), axis=0, keepdims=True)


def _col_sum(x):
    return jnp.sum(_col_reduce(x, jnp.add), axis=0, keepdims=True)


def _da_kernel(q_ref, k_ref, vt_ref, bias_ref, lam_ref, g_ref, o_ref, sa_ref, sb_ref, acc_ref, *, tk, tq, lam_init):
    qi = pl.program_id(2)
    nk = k_ref.shape[1] // tk
    lane = lax.broadcasted_iota(jnp.int32, (tq, LANES), 1)
    qs = q_ref[0]
    zero = jnp.zeros_like(qs)
    q_maps = (jnp.where(lane < DA_QK_DIM, qs, zero), jnp.where(lane >= DA_QK_DIM, qs, zero))

    def put_scores(ki, dst):
        start = pl.multiple_of(ki * tk, tk)
        k = k_ref[0, pl.ds(start, tk), :]
        for mp in range(2):
            dst[mp] = _dot_nt(k, q_maps[mp])

    col_blocks = [slice(c * LANES, (c + 1) * LANES) for c in range(tq // LANES)]
    ones_rows = jnp.ones((16, tk), BF16)

    def add_bias(ki, buf):
        bidx = jnp.maximum(2 * qi - ki + 1, 0)
        tile_max = []
        for mp in range(2):
            parts = []
            for cols in col_blocks:
                s = buf[mp, :, cols] + bias_ref[0, bidx, :, cols]
                buf[mp, :, cols] = s
                parts.append(_col_max(s))
            tile_max.append(jnp.concatenate(parts, axis=1))
        return tuple(tile_max)

    def exp_pv(ki, buf, tile_max, stats):
        vt = jnp.concatenate([vt_ref[0, ki], ones_rows], axis=0)
        new_stats, pending = [], []
        for mp in range(2):
            m, l = stats[mp]
            m_new = jnp.maximum(m, tile_max[mp])
            alpha = jnp.exp2(m - m_new)
            p = jnp.concatenate([jnp.exp2(buf[mp, :, cols] - m_new[:, cols]).astype(BF16) for cols in col_blocks],
                                axis=1)
            pv = _dot(vt, p)
            new_stats.append((m_new, alpha * l + pv[DA_V_DIM:DA_V_DIM + 1, :]))
            pending.append((alpha, pv[:DA_V_DIM]))
        return tuple(new_stats), pending

    def accumulate(pending):
        for mp in range(2):
            alpha, pv = pending[mp]
            acc_ref[mp] = alpha * acc_ref[mp] + pv

    def pair(j, carry):
        stats, max_a = carry
        ka = 2 * j
        put_scores(ka + 1, sb_ref)
        stats, pend = exp_pv(ka, sa_ref, max_a, stats)
        max_b = add_bias(ka + 1, sb_ref)
        accumulate(pend)
        nxt = jnp.minimum(ka + 2, nk - 1)
        put_scores(nxt, sa_ref)
        stats, pend = exp_pv(ka + 1, sb_ref, max_b, stats)
        max_a = add_bias(nxt, sa_ref)
        accumulate(pend)
        return stats, max_a

    put_scores(0, sa_ref)
    acc_ref[...] = jnp.zeros_like(acc_ref)
    init = tuple((jnp.full((1, tq), NEG_INF, F32), jnp.zeros((1, tq), F32)) for _ in range(2))
    ((_, l1), (_, l2)), _ = lax.fori_loop(0, qi + 1, pair, (init, add_bias(0, sa_ref)))

    ll = lam_ref[...]
    lam = (jnp.exp(jnp.sum(ll[0:1] * ll[1:2], axis=-1, keepdims=True))
           - jnp.exp(jnp.sum(ll[2:3] * ll[3:4], axis=-1, keepdims=True)) + lam_init)
    o = acc_ref[0] * (1.0 / l1) - lam * (acc_ref[1] * (1.0 / l2))
    o = o * lax.rsqrt(jnp.mean(o * o, axis=0, keepdims=True) + RMS_EPS) * g_ref[...]
    o_ref[0] = (o * (1.0 - lam_init)).T.astype(o_ref.dtype)


def _da_attention(proj, vt, bias, lam_p, norm_g, lam_init):
    bsz, seq, _ = proj.shape
    tk, tq = bias.shape[-2:]
    nk = seq // tk
    kb = DA_COLS // LANES
    return pl.pallas_call(
        functools.partial(_da_kernel, tk=tk, tq=tq, lam_init=lam_init),
        grid=(DA_HEADS, bsz, seq // tq),
        in_specs=[pl.BlockSpec((1, tq, LANES), lambda h, b, i: (b, i, h)),
                  pl.BlockSpec((1, seq, LANES), lambda h, b, i: (b, 0, kb + h)),
                  pl.BlockSpec((1, nk, DA_V_DIM, tk), lambda h, b, i: (b, 0, h, 0)),
                  pl.BlockSpec((1, bias.shape[1], tk, tq), lambda h, b, i: (h, 0, 0, 0)),
                  pl.BlockSpec((4, DA_QK_DIM), lambda h, b, i: (0, 0)),
                  pl.BlockSpec((DA_V_DIM, 1), lambda h, b, i: (0, 0))],
        out_specs=pl.BlockSpec((1, tq, LANES), lambda h, b, i: (b, i, h)),
        out_shape=jax.ShapeDtypeStruct((bsz, seq, DA_COLS), BF16),
        scratch_shapes=[pltpu.VMEM((2, tk, tq), F32), pltpu.VMEM((2, tk, tq), F32),
                        pltpu.VMEM((2, DA_V_DIM, tq), F32)],
        compiler_params=_cparams(("arbitrary", "parallel", "arbitrary")),
        name="diff_attention",
    )(proj, proj, vt, bias, lam_p, norm_g)


def _sb_kernel(q_ref, k_ref, vt_ref, o_ref, za_ref, zb_ref, acc_ref, *, tk, tq):
    qi = pl.program_id(2)
    lane = lax.broadcasted_iota(jnp.int32, (tq, LANES), 1)
    qs = q_ref[0]
    zero = jnp.zeros_like(qs)
    q_heads = (jnp.where(lane < SB_DIM, qs, zero), jnp.where(lane >= SB_DIM, qs, zero))
    chan = lax.broadcasted_iota(jnp.int32, (LANES, tk), 0)
    head_chans = (chan < SB_DIM, chan >= SB_DIM)
    tri = jnp.where(lax.broadcasted_iota(jnp.int32, (tk, tk), 0) <= lax.broadcasted_iota(jnp.int32, (tk, tk), 1),
                    1.0, 0.0).astype(BF16)
    key = lax.broadcasted_iota(jnp.int32, (tk, tq), 0)
    query = lax.broadcasted_iota(jnp.int32, (tk, tq), 1)

    def put_scores(ki, dst):
        start = pl.multiple_of(ki * tk, tk)
        k = k_ref[0, pl.ds(start, tk), :]
        for h in range(2):
            dst[h] = _dot_nt(k, q_heads[h])

    def process(ki, src, cs, before):
        vt = vt_ref[0, ki]
        mass = []
        for h in range(2):
            z = src[h]
            neg_abs = pltpu.bitcast(pltpu.bitcast(z, jnp.int32) | SIGN_BIT, F32)
            drop = jnp.maximum(z, 0.0) + jnp.log2(1.0 + jnp.exp2(neg_abs))
            if before is not None:
                drop = jnp.where(before, drop, 0.0)
            mass.append(_dot(tri, drop.astype(BF16)) + cs[h])
        new_cs = []
        part = None
        for h in range(2):
            a = jnp.exp2(src[h] - mass[h])
            if before is not None:
                a = jnp.where(before, a, 0.0)
            new_cs.append(mass[h][0:1, :])
            pv = _dot(jnp.where(head_chans[h], vt, jnp.zeros_like(vt)), a.astype(BF16))
            part = pv if part is None else part + pv
        return tuple(new_cs), part

    kd = 2 * qi + 1
    put_scores(kd, za_ref)
    put_scores(kd - 1, zb_ref)
    cs = (jnp.zeros((1, tq), F32), jnp.zeros((1, tq), F32))
    cs, part_a = process(kd, za_ref, cs, key + tk < query)
    put_scores(jnp.maximum(kd - 2, 0), za_ref)
    cs, part_b = process(kd - 1, zb_ref, cs, key < query)
    acc_ref[...] = part_a + part_b

    def pair(it, cs):
        ka = 2 * qi - 1 - 2 * it
        put_scores(ka - 1, zb_ref)
        cs, part_a = process(ka, za_ref, cs, None)
        put_scores(jnp.maximum(ka - 2, 0), za_ref)
        cs, part_b = process(ka - 1, zb_ref, cs, None)
        acc_ref[...] += part_a + part_b
        return cs

    lax.fori_loop(0, qi, pair, cs)
    o_ref[0] = acc_ref[...].T.astype(o_ref.dtype)


def _sb_attention(proj, vt):
    bsz, seq, _ = proj.shape
    tk = vt.shape[-1]
    tq = min(2 * tk, seq)
    nk = seq // tk
    qb = 2 * DA_COLS // LANES
    kb = qb + SB_COLS // LANES
    vb = DA_COLS // LANES
    return pl.pallas_call(
        functools.partial(_sb_kernel, tk=tk, tq=tq),
        grid=(bsz, SB_COLS // LANES, seq // tq),
        in_specs=[pl.BlockSpec((1, tq, LANES), lambda b, h, i: (b, i, qb + h)),
                  pl.BlockSpec((1, seq, LANES), lambda b, h, i: (b, 0, kb + h)),
                  pl.BlockSpec((1, nk, LANES, tk), lambda b, h, i: (b, 0, vb + h, 0))],
        out_specs=pl.BlockSpec((1, tq, LANES), lambda b, h, i: (b, i, h)),
        out_shape=jax.ShapeDtypeStruct((bsz, seq, SB_COLS), BF16),
        scratch_shapes=[pltpu.VMEM((2, tk, tq), F32), pltpu.VMEM((2, tk, tq), F32), pltpu.VMEM((LANES, tq), F32)],
        compiler_params=_cparams(("parallel", "parallel", "arbitrary")),
        name="stick_breaking_attention",
    )(proj, proj, vt)


def _dil_kernel(q_ref, kc_ref, kp_ref, vc_ref, vp_ref, bias_ref, o_ref, lse_ref, *, nblk):
    cc = pl.program_id(2)
    n = DIL_KEYS
    first = jnp.where(cc == 0, 1, 0)
    heads = range(DIL_HEADS_PER_GROUP)
    hcols = [slice(hs * DIL_DIM, (hs + 1) * DIL_DIM) for hs in heads]

    def blocks(q_rows, prev_ref, prev_rows, cur_rows, variant):
        sp, sc = [], []
        for hs in heads:
            qb = q_ref[0, 0, q_rows, hcols[hs]]
            bias = bias_ref[0, hs, variant]
            sp.append(_dot_nt(qb, prev_ref[0, 0, prev_rows, hcols[hs]]) + bias[:, :n])
            sc.append(_dot_nt(qb, kc_ref[0, 0, cur_rows, hcols[hs]]) + bias[:, n:])
        m = [jnp.maximum(jnp.max(sp[hs], axis=-1, keepdims=True), jnp.max(sc[hs], axis=-1, keepdims=True))
             for hs in heads]
        outs = []
        for hs in heads:
            pp = jnp.exp2(sp[hs] - m[hs])
            pc = jnp.exp2(sc[hs] - m[hs])
            l = jnp.sum(pp, axis=-1, keepdims=True) + jnp.sum(pc, axis=-1, keepdims=True)
            vprev = (vp_ref if prev_ref is kp_ref else vc_ref)[0, 0, prev_rows, hcols[hs]]
            o = _dot(pp.astype(BF16), vprev) + _dot(pc.astype(BF16), vc_ref[0, 0, cur_rows, hcols[hs]])
            outs.append((o, l))
        for hs in heads:
            o, l = outs[hs]
            o_ref[0, 0, q_rows, hcols[hs]] = o * (1.0 / l)
            lse_ref[0, 0, q_rows, hcols[hs]] = jnp.broadcast_to(m[hs] + jnp.log2(l), (n, DIL_DIM))

    blocks(slice(0, n), kp_ref, slice(None), slice(0, n), first)

    def body(j, _):
        cur = pl.ds(pl.multiple_of(j * n, n), n)
        prev = pl.ds(pl.multiple_of((j - 1) * n, n), n)
        blocks(cur, kc_ref, prev, cur, 0)
        return 0

    lax.fori_loop(1, nblk, body, 0)


def _dil_attention(qkv, col_blk, bias, group):
    bsz, dil, length, _ = qkv.shape
    n = DIL_KEYS
    chunk = min(DIL_CHUNK, length)
    nblk = chunk // n
    w = DIL_COLS
    cur = lambda off: pl.BlockSpec((1, 1, chunk, w), lambda b, r, c: (b, r, c, col_blk + off))
    prev = lambda off: pl.BlockSpec((1, 1, n, w),
                                    lambda b, r, c: (b, r, jnp.maximum(c * nblk - 1, 0), col_blk + off))
    out_spec = pl.BlockSpec((1, 1, chunk, w), lambda b, r, c: (b, r, c, 0))
    shape = jax.ShapeDtypeStruct((bsz, dil, length, w), F32)
    return pl.pallas_call(
        functools.partial(_dil_kernel, nblk=nblk),
        grid=(bsz, dil, length // chunk),
        in_specs=[cur(0), cur(1), prev(1), cur(2), prev(2),
                  pl.BlockSpec((1, DIL_HEADS_PER_GROUP, 2, n, 2 * n), lambda b, r, c: (group, 0, 0, 0, 0))],
        out_specs=[out_spec, out_spec],
        out_shape=[shape, shape],
        compiler_params=_cparams(("parallel", "parallel", "arbitrary")),
        name="dilated_attention",
    )(qkv, qkv, qkv, qkv, qkv, bias)


def _merge_kernel(x_ref, xb_ref, oa_ref, ob_ref, o0_ref, l0_ref, o1_ref, l1_ref, o2_ref, l2_ref,
                  wa_ref, wb_ref, wc_ref, wg_ref, wo_ref, g_ref, b_ref,
                  y_ref, yb_ref, s1o, s1l, s2o, s2l, oc_ref, *, alpha, tm):
    d = D_MODEL

    def combine_head_slot(c):
        cols = slice(c * DIL_DIM, (c + 1) * DIL_DIM)
        for src, dst in ((o1_ref, s1o), (l1_ref, s1l), (o2_ref, s2o), (l2_ref, s2l)):
            dil = src.shape[1]
            rows = tm // dil
            for r in range(dil):
                dst[c, pl.ds(r, rows, stride=dil), :] = src[0, r, :, cols]
        l0 = l0_ref[0, 0, :, cols]
        l1 = s1l[c]
        l2 = s2l[c]
        mx = jnp.maximum(jnp.maximum(l0, l1), l2)
        w0 = jnp.exp2(l0 - mx)
        w1 = jnp.exp2(l1 - mx)
        w2 = jnp.exp2(l2 - mx)
        oc = (w0 * o0_ref[0, 0, :, cols] + w1 * s1o[c] + w2 * s2o[c]) * (1.0 / (w0 + w1 + w2))
        oc_ref[:, cols] = oc.astype(BF16)

    gate_a = _dot(xb_ref[...], wg_ref[:, 0:d])
    combine_head_slot(0)
    branch_a = _dot(oa_ref[...], wa_ref[...])
    combine_head_slot(1)
    gate_b = _dot(xb_ref[...], wg_ref[:, d:2 * d])
    combine_head_slot(2)
    branch_b = _dot(ob_ref[...], wb_ref[...])
    combine_head_slot(3)
    gate_c = _dot(xb_ref[...], wg_ref[:, 2 * d:3 * d])
    merged = _sigmoid(gate_a) * branch_a + _sigmoid(gate_b) * branch_b
    merged = merged + _sigmoid(gate_c) * _dot(oc_ref[...], wc_ref[...])
    mixed = _dot(merged.astype(BF16), wo_ref[...])
    y = _layer_norm(alpha * x_ref[...] + mixed, g_ref[...], b_ref[...])
    y_ref[...] = y
    yb_ref[...] = y.astype(BF16)


def _merge(x, xb, oa, ob, c0, c1, c2, wa, wb, wc, wg, wo, g, b, alpha):
    bsz, seq, d = x.shape
    tm = min(MERGE_TM, seq)
    row = lambda width: pl.BlockSpec((None, tm, width), lambda bb, i: (bb, i, 0))
    full = lambda arr: pl.BlockSpec(arr.shape, lambda bb, i: (0,) * arr.ndim)

    def strided(arr):
        dil = arr.shape[1]
        return pl.BlockSpec((1, dil, tm // dil, DIL_COLS), lambda bb, i: (bb, 0, i, 0))

    weights = (wa, wb, wc, wg, wo, g, b)
    return pl.pallas_call(
        functools.partial(_merge_kernel, alpha=alpha, tm=tm),
        grid=(bsz, seq // tm),
        in_specs=[row(d), row(d), row(DA_COLS), row(SB_COLS),
                  strided(c0[0]), strided(c0[1]), strided(c1[0]), strided(c1[1]),
                  strided(c2[0]), strided(c2[1])] + [full(a) for a in weights],
        out_specs=[row(d), row(d)],
        out_shape=[jax.ShapeDtypeStruct((bsz, seq, d), F32), jax.ShapeDtypeStruct((bsz, seq, d), BF16)],
        scratch_shapes=[pltpu.VMEM((DIL_HEADS_PER_GROUP, tm, DIL_DIM), F32) for _ in range(4)]
        + [pltpu.VMEM((tm, DIL_COLS), BF16)],
        compiler_params=_cparams(("parallel", "arbitrary")),
        name="merge_outproj_ln",
    )(x, xb, oa, ob, c0[0], c0[1], c1[0], c1[1], c2[0], c2[1], *weights)


def _mlp_kernel(x_ref, xb_ref, p_ref, wu_ref, wd_ref, wpg_ref, wp_ref, g_ref, b_ref,
                y_ref, yb_ref, acc_ref, *, alpha):
    f = pl.program_id(1)
    hid = jnp.maximum(_dot(xb_ref[...], wu_ref[...]), 0.0)
    part = _dot((hid * hid).astype(BF16), wd_ref[...])

    @pl.when(f == 0)
    def _():
        acc_ref[...] = part

    @pl.when(f > 0)
    def _():
        acc_ref[...] += part

    @pl.when(f == pl.num_programs(1) - 1)
    def _():
        ple = _sigmoid(_dot(xb_ref[...], wpg_ref[...])) * _dot(p_ref[...].astype(BF16), wp_ref[...])
        y = _layer_norm(alpha * x_ref[...] + acc_ref[...] + ple, g_ref[...], b_ref[...])
        y_ref[...] = y
        yb_ref[...] = y.astype(BF16)


def _mlp(x, xb, p, wu, wd, wpg, wp, g, b, alpha):
    n, d = x.shape
    tm = min(MLP_TM, n)
    tf = MLP_TF
    row = lambda width: pl.BlockSpec((tm, width), lambda i, f: (i, 0))
    full = lambda arr: pl.BlockSpec(arr.shape, lambda i, f: (0,) * arr.ndim)
    return pl.pallas_call(
        functools.partial(_mlp_kernel, alpha=alpha),
        grid=(n // tm, D_FF // tf),
        in_specs=[row(d), row(d), row(PLE_DIM),
                  pl.BlockSpec((d, tf), lambda i, f: (0, f)),
                  pl.BlockSpec((tf, d), lambda i, f: (f, 0)),
                  full(wpg), full(wp), full(g), full(b)],
        out_specs=[row(d), row(d)],
        out_shape=[jax.ShapeDtypeStruct((n, d), F32), jax.ShapeDtypeStruct((n, d), BF16)],
        scratch_shapes=[pltpu.VMEM((tm, d), F32)],
        compiler_params=_cparams(("parallel", "arbitrary")),
        name="mlp_ple_ln",
    )(x, xb, p, wu, wd, wpg, wp, g, b)


def kernel(x, p, w_in, da_lambda, da_norm, w_branch_da, w_branch_sb, w_branch_dil, w_out,
           ln1_g, ln1_b, w_up, w_down, w_ple_gate, w_ple, ln2_g, ln2_b, rel_bias):
    bsz, seq, d = x.shape
    depth = w_in.shape[0]
    n = bsz * seq
    alpha = (2 * depth) ** 0.25
    assert d == D_MODEL and all(wd // dl == DIL_KEYS for wd, dl in DIL_PAIRS)
    assert seq % (DIL_KEYS * DIL_PAIRS[-1][1]) == 0 and seq % min(ATT_TILE, seq) == 0

    qkv_a = 3 * DA_COLS
    qkv_b = 3 * SB_COLS
    c0 = qkv_a + qkv_b
    cw = DIL_GROUPS * DIL_COLS

    def dil_cols(g):
        return [w_in[:, :, c0 + t * cw + g * DIL_COLS: c0 + t * cw + (g + 1) * DIL_COLS] for t in range(3)]

    w_main = jnp.concatenate([w_in[:, :, :2 * DA_COLS], w_in[:, :, qkv_a:qkv_a + 2 * SB_COLS]] + dil_cols(0),
                             axis=-1).astype(BF16)
    w_vt = jnp.concatenate([w_in[:, :, 2 * DA_COLS:qkv_a], w_in[:, :, qkv_a + 2 * SB_COLS:c0]],
                           axis=-1).transpose(0, 2, 1).astype(BF16)
    main_c0 = 2 * DA_COLS + 2 * SB_COLS
    ones = lambda width: jnp.ones((width,), F32)
    qscale = lambda width, dh: jnp.full((width,), LOG2E * dh ** -0.5, F32)
    cs_main = jnp.concatenate([qscale(DA_COLS, DA_QK_DIM), ones(DA_COLS), qscale(SB_COLS, SB_DIM), ones(SB_COLS),
                               qscale(DIL_COLS, DIL_DIM), ones(2 * DIL_COLS)]).reshape(1, -1)
    cs_dil = jnp.concatenate([qscale(DIL_COLS, DIL_DIM), ones(2 * DIL_COLS)]).reshape(1, -1)
    w_g1 = jnp.concatenate(dil_cols(1), axis=-1).astype(BF16)
    w_g2 = jnp.concatenate(dil_cols(2), axis=-1).astype(BF16)
    w_gate = w_in[:, :, c0 + 3 * cw:].astype(BF16)
    wa, wb, wc, wo = (w.astype(BF16) for w in (w_branch_da, w_branch_sb, w_branch_dil, w_out))
    wu, wd, wpg, wp = (w.astype(BF16) for w in (w_up, w_down, w_ple_gate, w_ple))

    dmax = max(seq, REL_MAX_DIST + 1)
    buckets = _rel_bucket(jnp.arange(dmax, dtype=jnp.int32))
    thr = jnp.sum(buckets[None, :] < jnp.arange(REL_BUCKETS, dtype=jnp.int32)[:, None], axis=1).astype(jnp.int32)
    tile = min(ATT_TILE, seq)
    da_bias = _da_bias_tiles(thr, rel_bias, seq, tile, min(DA_TQ, seq))
    dil_bias = _dil_bias_tiles(thr, rel_bias)

    xb = x.astype(BF16)
    for i in range(depth):
        lam_init = 0.8 - 0.6 * math.exp(-0.3 * i)
        proj = _proj(xb, w_main[i], cs_main, w_main.shape[-1] // 2)
        vt = _proj_t(xb, w_vt[i], tile)
        proj_g1 = _proj_dil(xb, w_g1[i], cs_dil, DIL_PAIRS[1][1])
        proj_g2 = _proj_dil(xb, w_g2[i], cs_dil, DIL_PAIRS[2][1])
        oa = _da_attention(proj, vt, da_bias, da_lambda[i], da_norm[i].reshape(DA_V_DIM, 1), lam_init)
        ob = _sb_attention(proj, vt)
        cgrp0 = _dil_attention(proj.reshape(bsz, 1, seq, proj.shape[-1]), main_c0 // DIL_COLS, dil_bias, 0)
        cgrp1 = _dil_attention(proj_g1, 0, dil_bias, 1)
        cgrp2 = _dil_attention(proj_g2, 0, dil_bias, 2)
        x1, x1b = _merge(x, xb, oa, ob, cgrp0, cgrp1, cgrp2, wa[i], wb[i], wc[i], w_gate[i], wo[i],
                         ln1_g[i].reshape(1, d), ln1_b[i].reshape(1, d), alpha)
        x2, x2b = _mlp(x1.reshape(n, d), x1b.reshape(n, d), p[i].reshape(n, PLE_DIM), wu[i], wd[i], wpg[i], wp[i],
                       ln2_g[i].reshape(1, d), ln2_b[i].reshape(1, d), alpha)
        x = x2.reshape(bsz, seq, d)
        xb = x2b.reshape(bsz, seq, d)
    return x
```

```python
import functools
import math

import jax
import jax.numpy as jnp
from jax import lax
from jax.experimental import pallas as pl
from jax.experimental.pallas import tpu as pltpu

D_MODEL = 1024
DA_HEADS = 4
DA_QK_DIM = 64
DA_V_DIM = 2 * DA_QK_DIM
SB_HEADS = 8
SB_DIM = 64
DIL_PAIRS = ((128, 1), (512, 4), (2048, 16))
DIL_GROUPS = len(DIL_PAIRS)
DIL_HEADS_PER_GROUP = 4
DIL_DIM = 128
DIL_KEYS = 128
D_FF = 4 * D_MODEL
PLE_DIM = 256
REL_BUCKETS = 32
REL_MAX_DIST = 2048
BIAS_HEADS = DA_HEADS + DIL_GROUPS * DIL_HEADS_PER_GROUP
LN_EPS = 1e-5
RMS_EPS = 1e-5

DA_COLS = DA_HEADS * 2 * DA_QK_DIM
SB_COLS = SB_HEADS * SB_DIM
DIL_COLS = DIL_HEADS_PER_GROUP * DIL_DIM
GATE_COLS = 3 * D_MODEL

LANES = 128
VMEM_LIMIT = 48 * 1024 * 1024

ATT_TILE = 256
DA_TQ = 2 * ATT_TILE
PROJ_TM = 1024
MERGE_TM = 256
MLP_TM = 512
MLP_TF = 1024
DIL_CHUNK = 1024

F32 = jnp.float32
BF16 = jnp.bfloat16
NEG_INF = float("-inf")
SIGN_BIT = -2 ** 31
LOG2E = math.log2(math.e)


def _cparams(sem):
    return pltpu.CompilerParams(dimension_semantics=("arbitrary",) * len(sem), vmem_limit_bytes=VMEM_LIMIT)


def _dot(a, b):
    return jnp.dot(a, b, preferred_element_type=F32)


def _dot_nt(a, b):
    return lax.dot_general(a, b, (((1,), (1,)), ((), ())), preferred_element_type=F32)


def _sigmoid(v):
    return 1.0 / (1.0 + jnp.exp(-v))


def _layer_norm(y, g, b):
    mu = jnp.mean(y, axis=-1, keepdims=True)
    yc = y - mu
    var = jnp.mean(yc * yc, axis=-1, keepdims=True)
    return yc * lax.rsqrt(var + LN_EPS) * g + b


def _rel_bucket(dist):
    max_exact = REL_BUCKETS // 2
    d = jnp.maximum(dist, 0)
    log_ratio = jnp.log(jnp.maximum(d, 1).astype(F32) / max_exact) / math.log(REL_MAX_DIST / max_exact)
    large = max_exact + (log_ratio * (REL_BUCKETS - max_exact)).astype(jnp.int32)
    return jnp.where(d < max_exact, d, jnp.minimum(large, REL_BUCKETS - 1))


def _bias_lookup(d, head, thr_ref, tab_ref):
    val = jnp.full(d.shape, tab_ref[0, head], F32)
    for k in range(1, REL_BUCKETS):
        val = jnp.where(d >= thr_ref[k], tab_ref[k, head], val)
    return val


def _da_bias_kernel(thr_ref, tab_ref, o_ref, *, tk, tq):
    h = pl.program_id(0)
    delta = pl.program_id(1) - 1
    key = lax.broadcasted_iota(jnp.int32, (tk, tq), 0)
    query = lax.broadcasted_iota(jnp.int32, (tk, tq), 1)
    d = delta * tk + query - key
    val = _bias_lookup(d, h, thr_ref, tab_ref) * LOG2E
    o_ref[0, 0] = jnp.where(d >= 0, val, NEG_INF)


def _da_bias_tiles(thr, rel_bias, seq, tk, tq):
    assert tq == 2 * tk
    nd = 2 * (seq // tq)
    return pl.pallas_call(
        functools.partial(_da_bias_kernel, tk=tk, tq=tq),
        grid=(DA_HEADS, nd),
        in_specs=[pl.BlockSpec(memory_space=pltpu.SMEM), pl.BlockSpec(memory_space=pltpu.SMEM)],
        out_specs=pl.BlockSpec((1, 1, tk, tq), lambda h, d: (h, d, 0, 0)),
        out_shape=jax.ShapeDtypeStruct((DA_HEADS, nd, tk, tq), F32),
        compiler_params=_cparams(("arbitrary", "arbitrary")),
        name="da_bias_tiles",
    )(thr, rel_bias)


def _dil_bias_kernel(thr_ref, tab_ref, o_ref):
    g = pl.program_id(0)
    hs = pl.program_id(1)
    n = DIL_KEYS
    dil = jnp.where(g == 0, DIL_PAIRS[0][1], jnp.where(g == 1, DIL_PAIRS[1][1], DIL_PAIRS[2][1]))
    row = lax.broadcasted_iota(jnp.int32, (n, 2 * n), 0)
    col = lax.broadcasted_iota(jnp.int32, (n, 2 * n), 1)
    step = row + n - col
    head = DA_HEADS + g * DIL_HEADS_PER_GROUP + hs
    val = _bias_lookup(step * dil, head, thr_ref, tab_ref) * LOG2E
    valid = jnp.where(step >= 0, jnp.where(step <= n, 1, 0), 0)
    o_ref[0, 0, 0] = jnp.where(valid > 0, val, NEG_INF)
    o_ref[0, 0, 1] = jnp.where(jnp.where(col >= n, valid, 0) > 0, val, NEG_INF)


def _dil_bias_tiles(thr, rel_bias):
    n = DIL_KEYS
    return pl.pallas_call(
        _dil_bias_kernel,
        grid=(DIL_GROUPS, DIL_HEADS_PER_GROUP),
        in_specs=[pl.BlockSpec(memory_space=pltpu.SMEM), pl.BlockSpec(memory_space=pltpu.SMEM)],
        out_specs=pl.BlockSpec((1, 1, 2, n, 2 * n), lambda g, h: (g, h, 0, 0, 0)),
        out_shape=jax.ShapeDtypeStruct((DIL_GROUPS, DIL_HEADS_PER_GROUP, 2, n, 2 * n), F32),
        compiler_params=_cparams(("arbitrary", "arbitrary")),
        name="dil_bias_tiles",
    )(thr, rel_bias)


def _proj_kernel(x_ref, w_ref, cs_ref, o_ref):
    o_ref[0] = (_dot(x_ref[0], w_ref[...]) * cs_ref[...]).astype(o_ref.dtype)


def _proj(xb, w, col_scale, tn):
    bsz, seq, d = xb.shape
    c = w.shape[1]
    tm = min(PROJ_TM, seq)
    return pl.pallas_call(
        _proj_kernel,
        grid=(bsz, seq // tm, c // tn),
        in_specs=[pl.BlockSpec((1, tm, d), lambda b, i, j: (b, i, 0)),
                  pl.BlockSpec((d, tn), lambda b, i, j: (0, j)),
                  pl.BlockSpec((1, tn), lambda b, i, j: (0, j))],
        out_specs=pl.BlockSpec((1, tm, tn), lambda b, i, j: (b, i, j)),
        out_shape=jax.ShapeDtypeStruct((bsz, seq, c), BF16),
        compiler_params=_cparams(("parallel", "parallel", "arbitrary")),
        name="proj",
    )(xb, w, col_scale)


def _proj_t_kernel(x_ref, wt_ref, o_ref, *, tile):
    res = _dot_nt(wt_ref[...], x_ref[0]).astype(o_ref.dtype)
    for c in range(o_ref.shape[1]):
        o_ref[0, c] = res[:, c * tile:(c + 1) * tile]


def _proj_t(xb, wt, tile):
    bsz, seq, d = xb.shape
    c = wt.shape[0]
    tm = min(PROJ_TM, seq)
    return pl.pallas_call(
        functools.partial(_proj_t_kernel, tile=tile),
        grid=(bsz, seq // tm),
        in_specs=[pl.BlockSpec((1, tm, d), lambda b, i: (b, i, 0)),
                  pl.BlockSpec((c, d), lambda b, i: (0, 0))],
        out_specs=pl.BlockSpec((1, tm // tile, c, tile), lambda b, i: (b, i, 0, 0)),
        out_shape=jax.ShapeDtypeStruct((bsz, seq // tile, c, tile), BF16),
        compiler_params=_cparams(("parallel", "arbitrary")),
        name="proj_t",
    )(xb, wt)


def _proj_dil_kernel(x_ref, w_ref, cs_ref, o_ref, res_ref, *, dil, rows):
    res = _dot(x_ref[0], w_ref[...]) * cs_ref[...]
    nblk = res.shape[1] // LANES
    for c in range(nblk):
        res_ref[c] = res[:, c * LANES:(c + 1) * LANES]
    for r in range(dil):
        for c in range(nblk):
            o_ref[0, r, :, c * LANES:(c + 1) * LANES] = (
                res_ref[c, pl.ds(r, rows, stride=dil), :].astype(o_ref.dtype))


def _proj_dil(xb, w, col_scale, dil):
    bsz, seq, d = xb.shape
    c = w.shape[1]
    tm = min(PROJ_TM, seq)
    rows = tm // dil
    return pl.pallas_call(
        functools.partial(_proj_dil_kernel, dil=dil, rows=rows),
        grid=(bsz, seq // tm),
        in_specs=[pl.BlockSpec((1, tm, d), lambda b, i: (b, i, 0)),
                  pl.BlockSpec((d, c), lambda b, i: (0, 0)),
                  pl.BlockSpec((1, c), lambda b, i: (0, 0))],
        out_specs=pl.BlockSpec((1, dil, rows, c), lambda b, i: (b, 0, i, 0)),
        out_shape=jax.ShapeDtypeStruct((bsz, dil, seq // dil, c), BF16),
        scratch_shapes=[pltpu.VMEM((c // LANES, tm, LANES), F32)],
        compiler_params=_cparams(("parallel", "arbitrary")),
        name="proj_dil",
    )(xb, w, col_scale)


def _col_reduce(x, op):
    parts = [x[i:i + 8] for i in range(0, x.shape[0], 8)]
    while len(parts) > 1:
        parts = [op(parts[i], parts[i + 1]) for i in range(0, len(parts), 2)]
    return parts[0]


def _col_max(x):
    return jnp.max(_col_reduce(x, jnp.maximum), axis=0, keepdims=True)


def _col_sum(x):
    return jnp.sum(_col_reduce(x, jnp.add), axis=0, keepdims=True)


def _da_kernel(q_ref, k_ref, vt_ref, bias_ref, lam_ref, g_ref, o_ref, sa_ref, sb_ref, acc_ref, *, tk, tq, lam_init):
    qi = pl.program_id(2)
    nk = k_ref.shape[1] // tk
    lane = lax.broadcasted_iota(jnp.int32, (tq, LANES), 1)
    qs = q_ref[0]
    zero = jnp.zeros_like(qs)
    q_maps = (jnp.where(lane < DA_QK_DIM, qs, zero), jnp.where(lane >= DA_QK_DIM, qs, zero))

    def put_scores(ki, dst):
        start = pl.multiple_of(ki * tk, tk)
        k = k_ref[0, pl.ds(start, tk), :]
        for mp in range(2):
            dst[mp] = _dot_nt(k, q_maps[mp])

    col_blocks = [slice(c * LANES, (c + 1) * LANES) for c in range(tq // LANES)]
    ones_rows = jnp.ones((16, tk), BF16)

    def add_bias(ki, buf):
        bidx = jnp.maximum(2 * qi - ki + 1, 0)
        tile_max = []
        for mp in range(2):
            parts = []
            for cols in col_blocks:
                s = buf[mp, :, cols] + bias_ref[0, bidx, :, cols]
                buf[mp, :, cols] = s
                parts.append(_col_max(s))
            tile_max.append(jnp.concatenate(parts, axis=1))
        return tuple(tile_max)

    def exp_pv(ki, buf, tile_max, stats):
        vt = jnp.concatenate([vt_ref[0, ki], ones_rows], axis=0)
        new_stats, pending = [], []
        for mp in range(2):
            m, l = stats[mp]
            m_new = jnp.maximum(m, tile_max[mp])
            alpha = jnp.exp2(m - m_new)
            p = jnp.concatenate([jnp.exp2((buf[mp, :, cols] - m_new[:, cols]).astype(BF16)) for cols in col_blocks],
                                axis=1)
            pv = _dot(vt, p)
            new_stats.append((m_new, alpha * l + pv[DA_V_DIM:DA_V_DIM + 1, :]))
            pending.append((alpha, pv[:DA_V_DIM]))
        return tuple(new_stats), pending

    def accumulate(pending):
        for mp in range(2):
            alpha, pv = pending[mp]
            acc_ref[mp] = alpha * acc_ref[mp] + pv

    def pair(j, carry):
        stats, max_a = carry
        ka = 2 * j
        put_scores(ka + 1, sb_ref)
        stats, pend = exp_pv(ka, sa_ref, max_a, stats)
        max_b = add_bias(ka + 1, sb_ref)
        accumulate(pend)
        nxt = jnp.minimum(ka + 2, nk - 1)
        put_scores(nxt, sa_ref)
        stats, pend = exp_pv(ka + 1, sb_ref, max_b, stats)
        max_a = add_bias(nxt, sa_ref)
        accumulate(pend)
        return stats, max_a

    put_scores(0, sa_ref)
    acc_ref[...] = jnp.zeros_like(acc_ref)
    init = tuple((jnp.full((1, tq), NEG_INF, F32), jnp.zeros((1, tq), F32)) for _ in range(2))
    ((_, l1), (_, l2)), _ = lax.fori_loop(0, qi + 1, pair, (init, add_bias(0, sa_ref)))

    ll = lam_ref[...]
    lam = (jnp.exp(jnp.sum(ll[0:1] * ll[1:2], axis=-1, keepdims=True))
           - jnp.exp(jnp.sum(ll[2:3] * ll[3:4], axis=-1, keepdims=True)) + lam_init)
    o = acc_ref[0] * (1.0 / l1) - lam * (acc_ref[1] * (1.0 / l2))
    o = o * lax.rsqrt(jnp.mean(o * o, axis=0, keepdims=True) + RMS_EPS) * g_ref[...]
    o_ref[0] = (o * (1.0 - lam_init)).T.astype(o_ref.dtype)


def _da_attention(proj, vt, bias, lam_p, norm_g, lam_init):
    bsz, seq, _ = proj.shape
    tk, tq = bias.shape[-2:]
    nk = seq // tk
    kb = DA_COLS // LANES
    return pl.pallas_call(
        functools.partial(_da_kernel, tk=tk, tq=tq, lam_init=lam_init),
        grid=(DA_HEADS, bsz, seq // tq),
        in_specs=[pl.BlockSpec((1, tq, LANES), lambda h, b, i: (b, i, h)),
                  pl.BlockSpec((1, seq, LANES), lambda h, b, i: (b, 0, kb + h)),
                  pl.BlockSpec((1, nk, DA_V_DIM, tk), lambda h, b, i: (b, 0, h, 0)),
                  pl.BlockSpec((1, bias.shape[1], tk, tq), lambda h, b, i: (h, 0, 0, 0)),
                  pl.BlockSpec((4, DA_QK_DIM), lambda h, b, i: (0, 0)),
                  pl.BlockSpec((DA_V_DIM, 1), lambda h, b, i: (0, 0))],
        out_specs=pl.BlockSpec((1, tq, LANES), lambda h, b, i: (b, i, h)),
        out_shape=jax.ShapeDtypeStruct((bsz, seq, DA_COLS), BF16),
        scratch_shapes=[pltpu.VMEM((2, tk, tq), F32), pltpu.VMEM((2, tk, tq), F32),
                        pltpu.VMEM((2, DA_V_DIM, tq), F32)],
        compiler_params=_cparams(("arbitrary", "parallel", "arbitrary")),
        name="diff_attention",
    )(proj, proj, vt, bias, lam_p, norm_g)


def _sb_kernel(q_ref, k_ref, vt_ref, o_ref, za_ref, zb_ref, acc_ref, *, tk, tq):
    qi = pl.program_id(2)
    lane = lax.broadcasted_iota(jnp.int32, (tq, LANES), 1)
    qs = q_ref[0]
    zero = jnp.zeros_like(qs)
    q_heads = (jnp.where(lane < SB_DIM, qs, zero), jnp.where(lane >= SB_DIM, qs, zero))
    chan = lax.broadcasted_iota(jnp.int32, (LANES, tk), 0)
    head_chans = (chan < SB_DIM, chan >= SB_DIM)
    tri = jnp.where(lax.broadcasted_iota(jnp.int32, (tk, tk), 0) <= lax.broadcasted_iota(jnp.int32, (tk, tk), 1),
                    1.0, 0.0).astype(BF16)
    key = lax.broadcasted_iota(jnp.int32, (tk, tq), 0)
    query = lax.broadcasted_iota(jnp.int32, (tk, tq), 1)

    def put_scores(ki, dst):
        start = pl.multiple_of(ki * tk, tk)
        k = k_ref[0, pl.ds(start, tk), :]
        for h in range(2):
            dst[h] = _dot_nt(k, q_heads[h])

    def process(ki, src, cs, before):
        vt = vt_ref[0, ki]
        mass = []
        for h in range(2):
            z = src[h]
            neg_abs = pltpu.bitcast(pltpu.bitcast(z, jnp.int32) | SIGN_BIT, F32)
            drop = jnp.maximum(z, 0.0) + jnp.log2(1.0 + jnp.exp2(neg_abs))
            if before is not None:
                drop = jnp.where(before, drop, 0.0)
            mass.append(_dot(tri, drop.astype(BF16)) + cs[h])
        new_cs = []
        part = None
        for h in range(2):
            a = jnp.exp2(src[h] - mass[h])
            if before is not None:
                a = jnp.where(before, a, 0.0)
            new_cs.append(mass[h][0:1, :])
            pv = _dot(jnp.where(head_chans[h], vt, jnp.zeros_like(vt)), a.astype(BF16))
            part = pv if part is None else part + pv
        return tuple(new_cs), part

    kd = 2 * qi + 1
    put_scores(kd, za_ref)
    put_scores(kd - 1, zb_ref)
    cs = (jnp.zeros((1, tq), F32), jnp.zeros((1, tq), F32))
    cs, part_a = process(kd, za_ref, cs, key + tk < query)
    put_scores(jnp.maximum(kd - 2, 0), za_ref)
    cs, part_b = process(kd - 1, zb_ref, cs, key < query)
    acc_ref[...] = part_a + part_b

    def pair(it, cs):
        ka = 2 * qi - 1 - 2 * it
        put_scores(ka - 1, zb_ref)
        cs, part_a = process(ka, za_ref, cs, None)
        put_scores(jnp.maximum(ka - 2, 0), za_ref)
        cs, part_b = process(ka - 1, zb_ref, cs, None)
        acc_ref[...] += part_a + part_b
        return cs

    lax.fori_loop(0, qi, pair, cs)
    o_ref[0] = acc_ref[...].T.astype(o_ref.dtype)


def _sb_attention(proj, vt):
    bsz, seq, _ = proj.shape
    tk = vt.shape[-1]
    tq = min(2 * tk, seq)
    nk = seq // tk
    qb = 2 * DA_COLS // LANES
    kb = qb + SB_COLS // LANES
    vb = DA_COLS // LANES
    return pl.pallas_call(
        functools.partial(_sb_kernel, tk=tk, tq=tq),
        grid=(bsz, SB_COLS // LANES, seq // tq),
        in_specs=[pl.BlockSpec((1, tq, LANES), lambda b, h, i: (b, i, qb + h)),
                  pl.BlockSpec((1, seq, LANES), lambda b, h, i: (b, 0, kb + h)),
                  pl.BlockSpec((1, nk, LANES, tk), lambda b, h, i: (b, 0, vb + h, 0))],
        out_specs=pl.BlockSpec((1, tq, LANES), lambda b, h, i: (b, i, h)),
        out_shape=jax.ShapeDtypeStruct((bsz, seq, SB_COLS), BF16),
        scratch_shapes=[pltpu.VMEM((2, tk, tq), F32), pltpu.VMEM((2, tk, tq), F32), pltpu.VMEM((LANES, tq), F32)],
        compiler_params=_cparams(("parallel", "parallel", "arbitrary")),
        name="stick_breaking_attention",
    )(proj, proj, vt)


def _dil_kernel(q_ref, kc_ref, kp_ref, vc_ref, vp_ref, bias_ref, o_ref, lse_ref, *, nblk):
    cc = pl.program_id(2)
    n = DIL_KEYS
    first = jnp.where(cc == 0, 1, 0)
    heads = range(DIL_HEADS_PER_GROUP)
    hcols = [slice(hs * DIL_DIM, (hs + 1) * DIL_DIM) for hs in heads]
    ones = jnp.ones((n, DIL_DIM), BF16)

    def blocks(descs):
        work = [(d, hs) for d in descs for hs in heads]
        sp, sc = [], []
        for (rows, kprev, _, prev_rows, variant), hs in work:
            qb = q_ref[0, 0, rows, hcols[hs]]
            bias = bias_ref[0, hs, variant]
            sp.append(_dot_nt(qb, kprev[0, 0, prev_rows, hcols[hs]]) + bias[:, :n])
            sc.append(_dot_nt(qb, kc_ref[0, 0, rows, hcols[hs]]) + bias[:, n:])
        m = [jnp.max(jnp.maximum(sp[i], sc[i]), axis=-1, keepdims=True) for i in range(len(work))]
        outs = []
        for i, ((rows, _, vprev, prev_rows, _), hs) in enumerate(work):
            pp = jnp.exp2(sp[i] - m[i]).astype(BF16)
            pc = jnp.exp2(sc[i] - m[i]).astype(BF16)
            outs.append(_dot(pp, jnp.concatenate([vprev[0, 0, prev_rows, hcols[hs]], ones], axis=1))
                        + _dot(pc, jnp.concatenate([vc_ref[0, 0, rows, hcols[hs]], ones], axis=1)))
        for i, ((rows, _, _, _, _), hs) in enumerate(work):
            o, l = outs[i][:, :DIL_DIM], outs[i][:, DIL_DIM:]
            o_ref[0, 0, rows, hcols[hs]] = o * (1.0 / l)
            lse_ref[0, 0, rows, hcols[hs]] = m[i] + jnp.log2(l)

    def desc(j):
        rows = slice(j * n, (j + 1) * n)
        if j == 0:
            return rows, kp_ref, vp_ref, slice(None), first
        return rows, kc_ref, vc_ref, slice((j - 1) * n, j * n), 0

    for j in range(0, nblk, 2):
        blocks([desc(jj) for jj in range(j, min(j + 2, nblk))])


def _dil_attention(qkv, col_blk, bias, group):
    bsz, dil, length, _ = qkv.shape
    n = DIL_KEYS
    chunk = min(DIL_CHUNK, length)
    nblk = chunk // n
    w = DIL_COLS
    cur = lambda off: pl.BlockSpec((1, 1, chunk, w), lambda b, r, c: (b, r, c, col_blk + off))
    prev = lambda off: pl.BlockSpec((1, 1, n, w),
                                    lambda b, r, c: (b, r, jnp.maximum(c * nblk - 1, 0), col_blk + off))
    out_spec = pl.BlockSpec((1, 1, chunk, w), lambda b, r, c: (b, r, c, 0))
    shape = jax.ShapeDtypeStruct((bsz, dil, length, w), F32)
    return pl.pallas_call(
        functools.partial(_dil_kernel, nblk=nblk),
        grid=(bsz, dil, length // chunk),
        in_specs=[cur(0), cur(1), prev(1), cur(2), prev(2),
                  pl.BlockSpec((1, DIL_HEADS_PER_GROUP, 2, n, 2 * n), lambda b, r, c: (group, 0, 0, 0, 0))],
        out_specs=[out_spec, out_spec],
        out_shape=[shape, shape],
        compiler_params=_cparams(("parallel", "parallel", "arbitrary")),
        name="dilated_attention",
    )(qkv, qkv, qkv, qkv, qkv, bias)


def _merge_kernel(x_ref, xb_ref, oa_ref, ob_ref, o0_ref, l0_ref, o1_ref, l1_ref, o2_ref, l2_ref,
                  wa_ref, wb_ref, wc_ref, wg_ref, wo_ref, g_ref, b_ref,
                  y_ref, yb_ref, s1o, s1l, s2o, s2l, oc_ref, *, alpha, tm):
    d = D_MODEL

    def combine_head_slot(c):
        cols = slice(c * DIL_DIM, (c + 1) * DIL_DIM)
        for src, dst in ((o1_ref, s1o), (l1_ref, s1l), (o2_ref, s2o), (l2_ref, s2l)):
            dil = src.shape[1]
            rows = tm // dil
            for r in range(dil):
                dst[c, pl.ds(r, rows, stride=dil), :] = src[0, r, :, cols]
        l0 = l0_ref[0, 0, :, cols]
        l1 = s1l[c]
        l2 = s2l[c]
        mx = jnp.maximum(jnp.maximum(l0, l1), l2)
        w0 = jnp.exp2(l0 - mx)
        w1 = jnp.exp2(l1 - mx)
        w2 = jnp.exp2(l2 - mx)
        oc = (w0 * o0_ref[0, 0, :, cols] + w1 * s1o[c] + w2 * s2o[c]) * (1.0 / (w0 + w1 + w2))
        oc_ref[:, cols] = oc.astype(BF16)

    gate_a = _dot(xb_ref[...], wg_ref[:, 0:d])
    combine_head_slot(0)
    branch_a = _dot(oa_ref[...], wa_ref[...])
    combine_head_slot(1)
    gate_b = _dot(xb_ref[...], wg_ref[:, d:2 * d])
    combine_head_slot(2)
    branch_b = _dot(ob_ref[...], wb_ref[...])
    combine_head_slot(3)
    gate_c = _dot(xb_ref[...], wg_ref[:, 2 * d:3 * d])
    merged = _sigmoid(gate_a) * branch_a + _sigmoid(gate_b) * branch_b
    merged = merged + _sigmoid(gate_c) * _dot(oc_ref[...], wc_ref[...])
    mixed = _dot(merged.astype(BF16), wo_ref[...])
    y = _layer_norm(alpha * x_ref[...] + mixed, g_ref[...], b_ref[...])
    y_ref[...] = y
    yb_ref[...] = y.astype(BF16)


def _merge(x, xb, oa, ob, c0, c1, c2, wa, wb, wc, wg, wo, g, b, alpha):
    bsz, seq, d = x.shape
    tm = min(MERGE_TM, seq)
    row = lambda width: pl.BlockSpec((None, tm, width), lambda bb, i: (bb, i, 0))
    full = lambda arr: pl.BlockSpec(arr.shape, lambda bb, i: (0,) * arr.ndim)

    def strided(arr):
        dil = arr.shape[1]
        return pl.BlockSpec((1, dil, tm // dil, DIL_COLS), lambda bb, i: (bb, 0, i, 0))

    weights = (wa, wb, wc, wg, wo, g, b)
    return pl.pallas_call(
        functools.partial(_merge_kernel, alpha=alpha, tm=tm),
        grid=(bsz, seq // tm),
        in_specs=[row(d), row(d), row(DA_COLS), row(SB_COLS),
                  strided(c0[0]), strided(c0[1]), strided(c1[0]), strided(c1[1]),
                  strided(c2[0]), strided(c2[1])] + [full(a) for a in weights],
        out_specs=[row(d), row(d)],
        out_shape=[jax.ShapeDtypeStruct((bsz, seq, d), F32), jax.ShapeDtypeStruct((bsz, seq, d), BF16)],
        scratch_shapes=[pltpu.VMEM((DIL_HEADS_PER_GROUP, tm, DIL_DIM), F32) for _ in range(4)]
        + [pltpu.VMEM((tm, DIL_COLS), BF16)],
        compiler_params=_cparams(("parallel", "arbitrary")),
        name="merge_outproj_ln",
    )(x, xb, oa, ob, c0[0], c0[1], c1[0], c1[1], c2[0], c2[1], *weights)


def _mlp_kernel(x_ref, xb_ref, p_ref, wu_ref, wd_ref, wpg_ref, wp_ref, g_ref, b_ref, y_ref, yb_ref, *, alpha, tf):
    xb = xb_ref[...]
    mlp = None
    for f in range(D_FF // tf):
        hid = jnp.maximum(_dot(xb, wu_ref[:, f * tf:(f + 1) * tf]), 0.0)
        part = _dot((hid * hid).astype(BF16), wd_ref[f * tf:(f + 1) * tf, :])
        mlp = part if mlp is None else mlp + part
    ple = _sigmoid(_dot(xb, wpg_ref[...])) * _dot(p_ref[...].astype(BF16), wp_ref[...])
    y = _layer_norm(alpha * x_ref[...] + mlp + ple, g_ref[...], b_ref[...])
    y_ref[...] = y
    yb_ref[...] = y.astype(BF16)


def _mlp(x, xb, p, wu, wd, wpg, wp, g, b, alpha):
    n, d = x.shape
    tm = min(MLP_TM, n)
    row = lambda width: pl.BlockSpec((tm, width), lambda i: (i, 0))
    full = lambda arr: pl.BlockSpec(arr.shape, lambda i: (0,) * arr.ndim, pipeline_mode=pl.Buffered(1))
    return pl.pallas_call(
        functools.partial(_mlp_kernel, alpha=alpha, tf=MLP_TF),
        grid=(n // tm,),
        in_specs=[row(d), row(d), row(PLE_DIM), full(wu), full(wd), full(wpg), full(wp), full(g), full(b)],
        out_specs=[row(d), row(d)],
        out_shape=[jax.ShapeDtypeStruct((n, d), F32), jax.ShapeDtypeStruct((n, d), BF16)],
        compiler_params=_cparams(("parallel",)),
        name="mlp_ple_ln",
    )(x, xb, p, wu, wd, wpg, wp, g, b)


def kernel(x, p, w_in, da_lambda, da_norm, w_branch_da, w_branch_sb, w_branch_dil, w_out,
           ln1_g, ln1_b, w_up, w_down, w_ple_gate, w_ple, ln2_g, ln2_b, rel_bias):
    bsz, seq, d = x.shape
    depth = w_in.shape[0]
    n = bsz * seq
    alpha = (2 * depth) ** 0.25
    assert d == D_MODEL and all(wd // dl == DIL_KEYS for wd, dl in DIL_PAIRS)
    assert seq % (DIL_KEYS * DIL_PAIRS[-1][1]) == 0 and seq % min(ATT_TILE, seq) == 0

    qkv_a = 3 * DA_COLS
    qkv_b = 3 * SB_COLS
    c0 = qkv_a + qkv_b
    cw = DIL_GROUPS * DIL_COLS

    def dil_cols(g):
        return [w_in[:, :, c0 + t * cw + g * DIL_COLS: c0 + t * cw + (g + 1) * DIL_COLS] for t in range(3)]

    w_main = jnp.concatenate([w_in[:, :, :2 * DA_COLS], w_in[:, :, qkv_a:qkv_a + 2 * SB_COLS]] + dil_cols(0),
                             axis=-1).astype(BF16)
    w_vt = jnp.concatenate([w_in[:, :, 2 * DA_COLS:qkv_a], w_in[:, :, qkv_a + 2 * SB_COLS:c0]],
                           axis=-1).transpose(0, 2, 1).astype(BF16)
    main_c0 = 2 * DA_COLS + 2 * SB_COLS
    ones = lambda width: jnp.ones((width,), F32)
    qscale = lambda width, dh: jnp.full((width,), LOG2E * dh ** -0.5, F32)
    cs_main = jnp.concatenate([qscale(DA_COLS, DA_QK_DIM), ones(DA_COLS), qscale(SB_COLS, SB_DIM), ones(SB_COLS),
                               qscale(DIL_COLS, DIL_DIM), ones(2 * DIL_COLS)]).reshape(1, -1)
    cs_dil = jnp.concatenate([qscale(DIL_COLS, DIL_DIM), ones(2 * DIL_COLS)]).reshape(1, -1)
    w_g1 = jnp.concatenate(dil_cols(1), axis=-1).astype(BF16)
    w_g2 = jnp.concatenate(dil_cols(2), axis=-1).astype(BF16)
    w_gate = w_in[:, :, c0 + 3 * cw:].astype(BF16)
    wa, wb, wc, wo = (w.astype(BF16) for w in (w_branch_da, w_branch_sb, w_branch_dil, w_out))
    wu, wd, wpg, wp = (w.astype(BF16) for w in (w_up, w_down, w_ple_gate, w_ple))

    dmax = max(seq, REL_MAX_DIST + 1)
    buckets = _rel_bucket(jnp.arange(dmax, dtype=jnp.int32))
    thr = jnp.sum(buckets[None, :] < jnp.arange(REL_BUCKETS, dtype=jnp.int32)[:, None], axis=1).astype(jnp.int32)
    tile = min(ATT_TILE, seq)
    da_bias = _da_bias_tiles(thr, rel_bias, seq, tile, min(DA_TQ, seq))
    dil_bias = _dil_bias_tiles(thr, rel_bias)

    xb = x.astype(BF16)
    for i in range(depth):
        lam_init = 0.8 - 0.6 * math.exp(-0.3 * i)
        proj = _proj(xb, w_main[i], cs_main, w_main.shape[-1] // 2)
        vt = _proj_t(xb, w_vt[i], tile)
        proj_g1 = _proj_dil(xb, w_g1[i], cs_dil, DIL_PAIRS[1][1])
        proj_g2 = _proj_dil(xb, w_g2[i], cs_dil, DIL_PAIRS[2][1])
        oa = _da_attention(proj, vt, da_bias, da_lambda[i], da_norm[i].reshape(DA_V_DIM, 1), lam_init)
        ob = _sb_attention(proj, vt)
        cgrp0 = _dil_attention(proj.reshape(bsz, 1, seq, proj.shape[-1]), main_c0 // DIL_COLS, dil_bias, 0)
        cgrp1 = _dil_attention(proj_g1, 0, dil_bias, 1)
        cgrp2 = _dil_attention(proj_g2, 0, dil_bias, 2)
        x1, x1b = _merge(x, xb, oa, ob, cgrp0, cgrp1, cgrp2, wa[i], wb[i], wc[i], w_gate[i], wo[i],
                         ln1_g[i].reshape(1, d), ln1_b[i].reshape(1, d), alpha)
        x2, x2b = _mlp(x1.reshape(n, d), x1b.reshape(n, d), p[i].reshape(n, PLE_DIM), wu[i], wd[i], wpg[i], wp[i],
                       ln2_g[i].reshape(1, d), ln2_b[i].reshape(1, d), alpha)
        x = x2.reshape(bsz, seq, d)
        xb = x2b.reshape(bsz, seq, d)
    return x
```

```python
import functools
import math

import jax
import jax.numpy as jnp
from jax import lax
from jax.experimental import pallas as pl
from jax.experimental.pallas import tpu as pltpu

D_MODEL = 1024
DA_HEADS = 4
DA_QK_DIM = 64
DA_V_DIM = 2 * DA_QK_DIM
SB_HEADS = 8
SB_DIM = 64
DIL_PAIRS = ((128, 1), (512, 4), (2048, 16))
DIL_GROUPS = len(DIL_PAIRS)
DIL_HEADS_PER_GROUP = 4
DIL_DIM = 128
DIL_KEYS = 128
D_FF = 4 * D_MODEL
PLE_DIM = 256
REL_BUCKETS = 32
REL_MAX_DIST = 2048
BIAS_HEADS = DA_HEADS + DIL_GROUPS * DIL_HEADS_PER_GROUP
LN_EPS = 1e-5
RMS_EPS = 1e-5

DA_COLS = DA_HEADS * 2 * DA_QK_DIM
SB_COLS = SB_HEADS * SB_DIM
DIL_COLS = DIL_HEADS_PER_GROUP * DIL_DIM
GATE_COLS = 3 * D_MODEL

LANES = 128
VMEM_LIMIT = 48 * 1024 * 1024

ATT_TILE = 256
DA_TQ = 2 * ATT_TILE
PROJ_TM = 1024
MERGE_TM = 256
MLP_TM = 512
MLP_TF = 1024
DIL_CHUNK = 1024

F32 = jnp.float32
BF16 = jnp.bfloat16
NEG_INF = float("-inf")
SIGN_BIT = -2 ** 31
SB_EXIT_BITS = 160.0
LOG2E = math.log2(math.e)


def _cparams(sem):
    return pltpu.CompilerParams(dimension_semantics=("arbitrary",) * len(sem), vmem_limit_bytes=VMEM_LIMIT)


def _dot(a, b):
    return jnp.dot(a, b, preferred_element_type=F32)


def _dot_nt(a, b):
    return lax.dot_general(a, b, (((1,), (1,)), ((), ())), preferred_element_type=F32)


def _sigmoid(v):
    return 1.0 / (1.0 + jnp.exp(-v))


def _layer_norm(y, g, b):
    mu = jnp.mean(y, axis=-1, keepdims=True)
    yc = y - mu
    var = jnp.mean(yc * yc, axis=-1, keepdims=True)
    return yc * lax.rsqrt(var + LN_EPS) * g + b


def _rel_bucket(dist):
    max_exact = REL_BUCKETS // 2
    d = jnp.maximum(dist, 0)
    log_ratio = jnp.log(jnp.maximum(d, 1).astype(F32) / max_exact) / math.log(REL_MAX_DIST / max_exact)
    large = max_exact + (log_ratio * (REL_BUCKETS - max_exact)).astype(jnp.int32)
    return jnp.where(d < max_exact, d, jnp.minimum(large, REL_BUCKETS - 1))


def _bias_lookup(d, head, thr_ref, tab_ref):
    val = jnp.full(d.shape, tab_ref[0, head], F32)
    for k in range(1, REL_BUCKETS):
        val = jnp.where(d >= thr_ref[k], tab_ref[k, head], val)
    return val


def _da_bias_kernel(thr_ref, tab_ref, o_ref, *, tk, tq):
    h = pl.program_id(0)
    delta = pl.program_id(1) - 1
    key = lax.broadcasted_iota(jnp.int32, (tk, tq), 0)
    query = lax.broadcasted_iota(jnp.int32, (tk, tq), 1)
    d = delta * tk + query - key
    val = _bias_lookup(d, h, thr_ref, tab_ref) * LOG2E
    o_ref[0, 0] = jnp.where(d >= 0, val, NEG_INF)


def _da_bias_tiles(thr, rel_bias, seq, tk, tq):
    assert tq == 2 * tk
    nd = 2 * (seq // tq)
    return pl.pallas_call(
        functools.partial(_da_bias_kernel, tk=tk, tq=tq),
        grid=(DA_HEADS, nd),
        in_specs=[pl.BlockSpec(memory_space=pltpu.SMEM), pl.BlockSpec(memory_space=pltpu.SMEM)],
        out_specs=pl.BlockSpec((1, 1, tk, tq), lambda h, d: (h, d, 0, 0)),
        out_shape=jax.ShapeDtypeStruct((DA_HEADS, nd, tk, tq), F32),
        compiler_params=_cparams(("arbitrary", "arbitrary")),
        name="da_bias_tiles",
    )(thr, rel_bias)


def _dil_bias_kernel(thr_ref, tab_ref, o_ref):
    g = pl.program_id(0)
    hs = pl.program_id(1)
    n = DIL_KEYS
    dil = jnp.where(g == 0, DIL_PAIRS[0][1], jnp.where(g == 1, DIL_PAIRS[1][1], DIL_PAIRS[2][1]))
    row = lax.broadcasted_iota(jnp.int32, (n, 2 * n), 0)
    col = lax.broadcasted_iota(jnp.int32, (n, 2 * n), 1)
    step = row + n - col
    head = DA_HEADS + g * DIL_HEADS_PER_GROUP + hs
    val = _bias_lookup(step * dil, head, thr_ref, tab_ref) * LOG2E
    valid = jnp.where(step >= 0, jnp.where(step <= n, 1, 0), 0)
    o_ref[0, 0, 0] = jnp.where(valid > 0, val, NEG_INF)
    o_ref[0, 0, 1] = jnp.where(jnp.where(col >= n, valid, 0) > 0, val, NEG_INF)


def _dil_bias_tiles(thr, rel_bias):
    n = DIL_KEYS
    return pl.pallas_call(
        _dil_bias_kernel,
        grid=(DIL_GROUPS, DIL_HEADS_PER_GROUP),
        in_specs=[pl.BlockSpec(memory_space=pltpu.SMEM), pl.BlockSpec(memory_space=pltpu.SMEM)],
        out_specs=pl.BlockSpec((1, 1, 2, n, 2 * n), lambda g, h: (g, h, 0, 0, 0)),
        out_shape=jax.ShapeDtypeStruct((DIL_GROUPS, DIL_HEADS_PER_GROUP, 2, n, 2 * n), F32),
        compiler_params=_cparams(("arbitrary", "arbitrary")),
        name="dil_bias_tiles",
    )(thr, rel_bias)


def _proj_kernel(x_ref, w_ref, cs_ref, o_ref):
    o_ref[0] = (_dot(x_ref[0], w_ref[...]) * cs_ref[...]).astype(o_ref.dtype)


def _proj(xb, w, col_scale, tn):
    bsz, seq, d = xb.shape
    c = w.shape[1]
    tm = min(PROJ_TM, seq)
    return pl.pallas_call(
        _proj_kernel,
        grid=(bsz, seq // tm, c // tn),
        in_specs=[pl.BlockSpec((1, tm, d), lambda b, i, j: (b, i, 0)),
                  pl.BlockSpec((d, tn), lambda b, i, j: (0, j)),
                  pl.BlockSpec((1, tn), lambda b, i, j: (0, j))],
        out_specs=pl.BlockSpec((1, tm, tn), lambda b, i, j: (b, i, j)),
        out_shape=jax.ShapeDtypeStruct((bsz, seq, c), BF16),
        compiler_params=_cparams(("parallel", "parallel", "arbitrary")),
        name="proj",
    )(xb, w, col_scale)


def _proj_t_kernel(x_ref, wt_ref, o_ref, *, tile):
    res = _dot_nt(wt_ref[...], x_ref[0]).astype(o_ref.dtype)
    for c in range(o_ref.shape[1]):
        o_ref[0, c] = res[:, c * tile:(c + 1) * tile]


def _proj_t(xb, wt, tile):
    bsz, seq, d = xb.shape
    c = wt.shape[0]
    tm = min(PROJ_TM, seq)
    return pl.pallas_call(
        functools.partial(_proj_t_kernel, tile=tile),
        grid=(bsz, seq // tm),
        in_specs=[pl.BlockSpec((1, tm, d), lambda b, i: (b, i, 0)),
                  pl.BlockSpec((c, d), lambda b, i: (0, 0))],
        out_specs=pl.BlockSpec((1, tm // tile, c, tile), lambda b, i: (b, i, 0, 0)),
        out_shape=jax.ShapeDtypeStruct((bsz, seq // tile, c, tile), BF16),
        compiler_params=_cparams(("parallel", "arbitrary")),
        name="proj_t",
    )(xb, wt)


def _proj_dil_kernel(x_ref, w_ref, cs_ref, o_ref, res_ref, *, dil, rows):
    res = _dot(x_ref[0], w_ref[...]) * cs_ref[...]
    nblk = res.shape[1] // LANES
    for c in range(nblk):
        res_ref[c] = res[:, c * LANES:(c + 1) * LANES]
    for r in range(dil):
        for c in range(nblk):
            o_ref[0, r, :, c * LANES:(c + 1) * LANES] = (
                res_ref[c, pl.ds(r, rows, stride=dil), :].astype(o_ref.dtype))


def _proj_dil(xb, w, col_scale, dil):
    bsz, seq, d = xb.shape
    c = w.shape[1]
    tm = min(PROJ_TM, seq)
    rows = tm // dil
    return pl.pallas_call(
        functools.partial(_proj_dil_kernel, dil=dil, rows=rows),
        grid=(bsz, seq // tm),
        in_specs=[pl.BlockSpec((1, tm, d), lambda b, i: (b, i, 0)),
                  pl.BlockSpec((d, c), lambda b, i: (0, 0)),
                  pl.BlockSpec((1, c), lambda b, i: (0, 0))],
        out_specs=pl.BlockSpec((1, dil, rows, c), lambda b, i: (b, 0, i, 0)),
        out_shape=jax.ShapeDtypeStruct((bsz, dil, seq // dil, c), BF16),
        scratch_shapes=[pltpu.VMEM((c // LANES, tm, LANES), F32)],
        compiler_params=_cparams(("parallel", "arbitrary")),
        name="proj_dil",
    )(xb, w, col_scale)


def _col_reduce(x, op):
    parts = [x[i:i + 8] for i in range(0, x.shape[0], 8)]
    while len(parts) > 1:
        parts = [op(parts[i], parts[i + 1]) for i in range(0, len(parts), 2)]
    return parts[0]


def _col_max(x):
    return jnp.max(_col_reduce(x, jnp.maximum), axis=0, keepdims=True)


def _col_sum(x):
    return jnp.sum(_col_reduce(x, jnp.add), axis=0, keepdims=True)


def _da_kernel(q_ref, k_ref, vt_ref, bias_ref, lam_ref, g_ref, o_ref, sa_ref, sb_ref, acc_ref, *, tk, tq, lam_init):
    qi = pl.program_id(2)
    nk = k_ref.shape[1] // tk
    lane = lax.broadcasted_iota(jnp.int32, (tq, LANES), 1)
    qs = q_ref[0]
    zero = jnp.zeros_like(qs)
    q_maps = (jnp.where(lane < DA_QK_DIM, qs, zero), jnp.where(lane >= DA_QK_DIM, qs, zero))

    def put_scores(ki, dst):
        start = pl.multiple_of(ki * tk, tk)
        k = k_ref[0, pl.ds(start, tk), :]
        for mp in range(2):
            dst[mp] = _dot_nt(k, q_maps[mp])

    col_blocks = [slice(c * LANES, (c + 1) * LANES) for c in range(tq // LANES)]
    ones_rows = jnp.ones((16, tk), BF16)

    def add_bias(ki, buf):
        bidx = jnp.maximum(2 * qi - ki + 1, 0)
        tile_max = []
        for mp in range(2):
            parts = []
            for cols in col_blocks:
                s = buf[mp, :, cols] + bias_ref[0, bidx, :, cols]
                buf[mp, :, cols] = s
                parts.append(_col_max(s))
            tile_max.append(jnp.concatenate(parts, axis=1))
        return tuple(tile_max)

    def exp_pv(ki, buf, tile_max, stats):
        vt = jnp.concatenate([vt_ref[0, ki], ones_rows], axis=0)
        new_stats, pending = [], []
        for mp in range(2):
            m, l = stats[mp]
            m_new = jnp.maximum(m, tile_max[mp])
            alpha = jnp.exp2(m - m_new)
            p = jnp.concatenate([jnp.exp2((buf[mp, :, cols] - m_new[:, cols]).astype(BF16)) for cols in col_blocks],
                                axis=1)
            pv = _dot(vt, p)
            new_stats.append((m_new, alpha * l + pv[DA_V_DIM:DA_V_DIM + 1, :]))
            pending.append((alpha, pv[:DA_V_DIM]))
        return tuple(new_stats), pending

    def accumulate(pending):
        for mp in range(2):
            alpha, pv = pending[mp]
            acc_ref[mp] = alpha * acc_ref[mp] + pv

    def pair(j, carry):
        stats, max_a = carry
        ka = 2 * j
        put_scores(ka + 1, sb_ref)
        stats, pend = exp_pv(ka, sa_ref, max_a, stats)
        max_b = add_bias(ka + 1, sb_ref)
        accumulate(pend)
        nxt = jnp.minimum(ka + 2, nk - 1)
        put_scores(nxt, sa_ref)
        stats, pend = exp_pv(ka + 1, sb_ref, max_b, stats)
        max_a = add_bias(nxt, sa_ref)
        accumulate(pend)
        return stats, max_a

    put_scores(0, sa_ref)
    acc_ref[...] = jnp.zeros_like(acc_ref)
    init = tuple((jnp.full((1, tq), NEG_INF, F32), jnp.zeros((1, tq), F32)) for _ in range(2))
    ((_, l1), (_, l2)), _ = lax.fori_loop(0, qi + 1, pair, (init, add_bias(0, sa_ref)))

    ll = lam_ref[...]
    lam = (jnp.exp(jnp.sum(ll[0:1] * ll[1:2], axis=-1, keepdims=True))
           - jnp.exp(jnp.sum(ll[2:3] * ll[3:4], axis=-1, keepdims=True)) + lam_init)
    o = acc_ref[0] * (1.0 / l1) - lam * (acc_ref[1] * (1.0 / l2))
    o = o * lax.rsqrt(jnp.mean(o * o, axis=0, keepdims=True) + RMS_EPS) * g_ref[...]
    o_ref[0] = (o * (1.0 - lam_init)).T.astype(o_ref.dtype)


def _da_attention(proj, vt, bias, lam_p, norm_g, lam_init):
    bsz, seq, _ = proj.shape
    tk, tq = bias.shape[-2:]
    nk = seq // tk
    kb = DA_COLS // LANES
    return pl.pallas_call(
        functools.partial(_da_kernel, tk=tk, tq=tq, lam_init=lam_init),
        grid=(DA_HEADS, bsz, seq // tq),
        in_specs=[pl.BlockSpec((1, tq, LANES), lambda h, b, i: (b, i, h)),
                  pl.BlockSpec((1, seq, LANES), lambda h, b, i: (b, 0, kb + h)),
                  pl.BlockSpec((1, nk, DA_V_DIM, tk), lambda h, b, i: (b, 0, h, 0)),
                  pl.BlockSpec((1, bias.shape[1], tk, tq), lambda h, b, i: (h, 0, 0, 0)),
                  pl.BlockSpec((4, DA_QK_DIM), lambda h, b, i: (0, 0)),
                  pl.BlockSpec((DA_V_DIM, 1), lambda h, b, i: (0, 0))],
        out_specs=pl.BlockSpec((1, tq, LANES), lambda h, b, i: (b, i, h)),
        out_shape=jax.ShapeDtypeStruct((bsz, seq, DA_COLS), BF16),
        scratch_shapes=[pltpu.VMEM((2, tk, tq), F32), pltpu.VMEM((2, tk, tq), F32),
                        pltpu.VMEM((2, DA_V_DIM, tq), F32)],
        compiler_params=_cparams(("arbitrary", "parallel", "arbitrary")),
        name="diff_attention",
    )(proj, proj, vt, bias, lam_p, norm_g)


def _sb_kernel(q_ref, k_ref, vt_ref, o_ref, za_ref, zb_ref, acc_ref, *, tk, tq):
    qi = pl.program_id(2)
    lane = lax.broadcasted_iota(jnp.int32, (tq, LANES), 1)
    qs = q_ref[0]
    zero = jnp.zeros_like(qs)
    q_heads = (jnp.where(lane < SB_DIM, qs, zero), jnp.where(lane >= SB_DIM, qs, zero))
    chan = lax.broadcasted_iota(jnp.int32, (LANES, tk), 0)
    head_chans = (chan < SB_DIM, chan >= SB_DIM)
    tri = jnp.where(lax.broadcasted_iota(jnp.int32, (tk, tk), 0) < lax.broadcasted_iota(jnp.int32, (tk, tk), 1),
                    1.0, 0.0).astype(BF16)
    key = lax.broadcasted_iota(jnp.int32, (tk, tq), 0)
    query = lax.broadcasted_iota(jnp.int32, (tk, tq), 1)

    def put_scores(ki, dst):
        start = pl.multiple_of(ki * tk, tk)
        k = k_ref[0, pl.ds(start, tk), :]
        for h in range(2):
            dst[h] = _dot_nt(k, q_heads[h])

    def process(ki, src, cs, before):
        vt = vt_ref[0, ki]
        later, own = [], []
        for h in range(2):
            z = src[h]
            neg_abs = pltpu.bitcast(pltpu.bitcast(z, jnp.int32) | SIGN_BIT, F32)
            drop = jnp.maximum(z, 0.0) + jnp.log2(1.0 + jnp.exp2(neg_abs))
            src[h] = z - drop
            if before is not None:
                drop = jnp.where(before, drop, 0.0)
            drop = drop.astype(BF16)
            later.append(_dot(tri, drop) + cs[h])
            own.append(drop[0:1, :].astype(F32))
        new_cs = []
        part = None
        for h in range(2):
            a = jnp.exp2(src[h] - later[h])
            if before is not None:
                a = jnp.where(before, a, 0.0)
            new_cs.append(later[h][0:1, :] + own[h])
            pv = _dot(jnp.where(head_chans[h], vt, jnp.zeros_like(vt)), a.astype(BF16))
            part = pv if part is None else part + pv
        return tuple(new_cs), part

    kd = 2 * qi + 1
    put_scores(kd, za_ref)
    put_scores(kd - 1, zb_ref)
    cs = (jnp.zeros((1, tq), F32), jnp.zeros((1, tq), F32))
    cs, part_a = process(kd, za_ref, cs, key + tk < query)
    put_scores(jnp.maximum(kd - 2, 0), za_ref)
    cs, part_b = process(kd - 1, zb_ref, cs, key < query)
    acc_ref[...] = part_a + part_b

    def least(cs):
        return jnp.min(jnp.minimum(cs[0], cs[1]))

    def pair(state):
        it, cs, _ = state
        ka = 2 * qi - 1 - 2 * it
        put_scores(ka - 1, zb_ref)
        cs, part_a = process(ka, za_ref, cs, None)
        put_scores(jnp.maximum(ka - 2, 0), za_ref)
        cs, part_b = process(ka - 1, zb_ref, cs, None)
        acc_ref[...] += part_a + part_b
        return it + 1, cs, least(cs)

    def more(state):
        it, _, cs_min = state
        return jnp.logical_and(it < qi, cs_min < SB_EXIT_BITS)

    lax.while_loop(more, pair, (jnp.int32(0), cs, least(cs)))
    o_ref[0] = acc_ref[...].T.astype(o_ref.dtype)


def _sb_attention(proj, vt):
    bsz, seq, _ = proj.shape
    tk = vt.shape[-1]
    tq = min(2 * tk, seq)
    nk = seq // tk
    qb = 2 * DA_COLS // LANES
    kb = qb + SB_COLS // LANES
    vb = DA_COLS // LANES
    return pl.pallas_call(
        functools.partial(_sb_kernel, tk=tk, tq=tq),
        grid=(bsz, SB_COLS // LANES, seq // tq),
        in_specs=[pl.BlockSpec((1, tq, LANES), lambda b, h, i: (b, i, qb + h)),
                  pl.BlockSpec((1, seq, LANES), lambda b, h, i: (b, 0, kb + h)),
                  pl.BlockSpec((1, nk, LANES, tk), lambda b, h, i: (b, 0, vb + h, 0))],
        out_specs=pl.BlockSpec((1, tq, LANES), lambda b, h, i: (b, i, h)),
        out_shape=jax.ShapeDtypeStruct((bsz, seq, SB_COLS), BF16),
        scratch_shapes=[pltpu.VMEM((2, tk, tq), F32), pltpu.VMEM((2, tk, tq), F32), pltpu.VMEM((LANES, tq), F32)],
        compiler_params=_cparams(("parallel", "parallel", "arbitrary")),
        name="stick_breaking_attention",
    )(proj, proj, vt)


def _dil_kernel(q_ref, kc_ref, kp_ref, vc_ref, vp_ref, bias_ref, o_ref, lse_ref, *, nblk):
    cc = pl.program_id(2)
    n = DIL_KEYS
    first = jnp.where(cc == 0, 1, 0)
    heads = range(DIL_HEADS_PER_GROUP)
    hcols = [slice(hs * DIL_DIM, (hs + 1) * DIL_DIM) for hs in heads]
    ones = jnp.ones((n, DIL_DIM), BF16)

    def blocks(descs):
        work = [(d, hs) for d in descs for hs in heads]
        sp, sc = [], []
        for (rows, kprev, _, prev_rows, variant), hs in work:
            qb = q_ref[0, 0, rows, hcols[hs]]
            bias = bias_ref[0, hs, variant]
            sp.append(_dot_nt(qb, kprev[0, 0, prev_rows, hcols[hs]]) + bias[:, :n])
            sc.append(_dot_nt(qb, kc_ref[0, 0, rows, hcols[hs]]) + bias[:, n:])
        m = [jnp.max(jnp.maximum(sp[i], sc[i]), axis=-1, keepdims=True) for i in range(len(work))]
        outs = []
        for i, ((rows, _, vprev, prev_rows, _), hs) in enumerate(work):
            pp = jnp.exp2(sp[i] - m[i]).astype(BF16)
            pc = jnp.exp2(sc[i] - m[i]).astype(BF16)
            outs.append(_dot(pp, jnp.concatenate([vprev[0, 0, prev_rows, hcols[hs]], ones], axis=1))
                        + _dot(pc, jnp.concatenate([vc_ref[0, 0, rows, hcols[hs]], ones], axis=1)))
        for i, ((rows, _, _, _, _), hs) in enumerate(work):
            o, l = outs[i][:, :DIL_DIM], outs[i][:, DIL_DIM:]
            o_ref[0, 0, rows, hcols[hs]] = o * (1.0 / l)
            lse_ref[0, 0, rows, hcols[hs]] = m[i] + jnp.log2(l)

    def desc(j):
        rows = slice(j * n, (j + 1) * n)
        if j == 0:
            return rows, kp_ref, vp_ref, slice(None), first
        return rows, kc_ref, vc_ref, slice((j - 1) * n, j * n), 0

    for j in range(0, nblk, 2):
        blocks([desc(jj) for jj in range(j, min(j + 2, nblk))])


def _dil_attention(qkv, col_blk, bias, group):
    bsz, dil, length, _ = qkv.shape
    n = DIL_KEYS
    chunk = min(DIL_CHUNK, length)
    nblk = chunk // n
    w = DIL_COLS
    cur = lambda off: pl.BlockSpec((1, 1, chunk, w), lambda b, r, c: (b, r, c, col_blk + off))
    prev = lambda off: pl.BlockSpec((1, 1, n, w),
                                    lambda b, r, c: (b, r, jnp.maximum(c * nblk - 1, 0), col_blk + off))
    out_spec = pl.BlockSpec((1, 1, chunk, w), lambda b, r, c: (b, r, c, 0))
    shape = jax.ShapeDtypeStruct((bsz, dil, length, w), F32)
    return pl.pallas_call(
        functools.partial(_dil_kernel, nblk=nblk),
        grid=(bsz, dil, length // chunk),
        in_specs=[cur(0), cur(1), prev(1), cur(2), prev(2),
                  pl.BlockSpec((1, DIL_HEADS_PER_GROUP, 2, n, 2 * n), lambda b, r, c: (group, 0, 0, 0, 0))],
        out_specs=[out_spec, out_spec],
        out_shape=[shape, shape],
        compiler_params=_cparams(("parallel", "parallel", "arbitrary")),
        name="dilated_attention",
    )(qkv, qkv, qkv, qkv, qkv, bias)


def _merge_kernel(x_ref, xb_ref, oa_ref, ob_ref, o0_ref, l0_ref, o1_ref, l1_ref, o2_ref, l2_ref,
                  wa_ref, wb_ref, wc_ref, wg_ref, wo_ref, g_ref, b_ref,
                  y_ref, yb_ref, s1o, s1l, s2o, s2l, oc_ref, *, alpha, tm):
    d = D_MODEL

    def combine_head_slot(c):
        cols = slice(c * DIL_DIM, (c + 1) * DIL_DIM)
        for src, dst in ((o1_ref, s1o), (l1_ref, s1l), (o2_ref, s2o), (l2_ref, s2l)):
            dil = src.shape[1]
            rows = tm // dil
            for r in range(dil):
                dst[c, pl.ds(r, rows, stride=dil), :] = src[0, r, :, cols]
        l0 = l0_ref[0, 0, :, cols]
        l1 = s1l[c]
        l2 = s2l[c]
        mx = jnp.maximum(jnp.maximum(l0, l1), l2)
        w0 = jnp.exp2(l0 - mx)
        w1 = jnp.exp2(l1 - mx)
        w2 = jnp.exp2(l2 - mx)
        oc = (w0 * o0_ref[0, 0, :, cols] + w1 * s1o[c] + w2 * s2o[c]) * (1.0 / (w0 + w1 + w2))
        oc_ref[:, cols] = oc.astype(BF16)

    gate_a = _dot(xb_ref[...], wg_ref[:, 0:d])
    combine_head_slot(0)
    branch_a = _dot(oa_ref[...], wa_ref[...])
    combine_head_slot(1)
    gate_b = _dot(xb_ref[...], wg_ref[:, d:2 * d])
    combine_head_slot(2)
    branch_b = _dot(ob_ref[...], wb_ref[...])
    combine_head_slot(3)
    gate_c = _dot(xb_ref[...], wg_ref[:, 2 * d:3 * d])
    merged = _sigmoid(gate_a) * branch_a + _sigmoid(gate_b) * branch_b
    merged = merged + _sigmoid(gate_c) * _dot(oc_ref[...], wc_ref[...])
    mixed = _dot(merged.astype(BF16), wo_ref[...])
    y = _layer_norm(alpha * x_ref[...] + mixed, g_ref[...], b_ref[...])
    y_ref[...] = y
    yb_ref[...] = y.astype(BF16)


def _merge(x, xb, oa, ob, c0, c1, c2, wa, wb, wc, wg, wo, g, b, alpha):
    bsz, seq, d = x.shape
    tm = min(MERGE_TM, seq)
    row = lambda width: pl.BlockSpec((None, tm, width), lambda bb, i: (bb, i, 0))
    full = lambda arr: pl.BlockSpec(arr.shape, lambda bb, i: (0,) * arr.ndim)

    def strided(arr):
        dil = arr.shape[1]
        return pl.BlockSpec((1, dil, tm // dil, DIL_COLS), lambda bb, i: (bb, 0, i, 0))

    weights = (wa, wb, wc, wg, wo, g, b)
    return pl.pallas_call(
        functools.partial(_merge_kernel, alpha=alpha, tm=tm),
        grid=(bsz, seq // tm),
        in_specs=[row(d), row(d), row(DA_COLS), row(SB_COLS),
                  strided(c0[0]), strided(c0[1]), strided(c1[0]), strided(c1[1]),
                  strided(c2[0]), strided(c2[1])] + [full(a) for a in weights],
        out_specs=[row(d), row(d)],
        out_shape=[jax.ShapeDtypeStruct((bsz, seq, d), F32), jax.ShapeDtypeStruct((bsz, seq, d), BF16)],
        scratch_shapes=[pltpu.VMEM((DIL_HEADS_PER_GROUP, tm, DIL_DIM), F32) for _ in range(4)]
        + [pltpu.VMEM((tm, DIL_COLS), BF16)],
        compiler_params=_cparams(("parallel", "arbitrary")),
        name="merge_outproj_ln",
    )(x, xb, oa, ob, c0[0], c0[1], c1[0], c1[1], c2[0], c2[1], *weights)


def _mlp_kernel(x_ref, xb_ref, p_ref, wu_ref, wd_ref, wpg_ref, wp_ref, g_ref, b_ref, y_ref, yb_ref, *, alpha, tf):
    xb = xb_ref[...]
    mlp = None
    for f in range(D_FF // tf):
        hid = jnp.maximum(_dot(xb, wu_ref[:, f * tf:(f + 1) * tf]), 0.0)
        part = _dot((hid * hid).astype(BF16), wd_ref[f * tf:(f + 1) * tf, :])
        mlp = part if mlp is None else mlp + part
    ple = _sigmoid(_dot(xb, wpg_ref[...])) * _dot(p_ref[...].astype(BF16), wp_ref[...])
    y = _layer_norm(alpha * x_ref[...] + mlp + ple, g_ref[...], b_ref[...])
    y_ref[...] = y
    yb_ref[...] = y.astype(BF16)


def _mlp(x, xb, p, wu, wd, wpg, wp, g, b, alpha):
    n, d = x.shape
    tm = min(MLP_TM, n)
    row = lambda width: pl.BlockSpec((tm, width), lambda i: (i, 0))
    full = lambda arr: pl.BlockSpec(arr.shape, lambda i: (0,) * arr.ndim, pipeline_mode=pl.Buffered(1))
    return pl.pallas_call(
        functools.partial(_mlp_kernel, alpha=alpha, tf=MLP_TF),
        grid=(n // tm,),
        in_specs=[row(d), row(d), row(PLE_DIM), full(wu), full(wd), full(wpg), full(wp), full(g), full(b)],
        out_specs=[row(d), row(d)],
        out_shape=[jax.ShapeDtypeStruct((n, d), F32), jax.ShapeDtypeStruct((n, d), BF16)],
        compiler_params=_cparams(("parallel",)),
        name="mlp_ple_ln",
    )(x, xb, p, wu, wd, wpg, wp, g, b)


def kernel(x, p, w_in, da_lambda, da_norm, w_branch_da, w_branch_sb, w_branch_dil, w_out,
           ln1_g, ln1_b, w_up, w_down, w_ple_gate, w_ple, ln2_g, ln2_b, rel_bias):
    bsz, seq, d = x.shape
    depth = w_in.shape[0]
    n = bsz * seq
    alpha = (2 * depth) ** 0.25
    assert d == D_MODEL and all(wd // dl == DIL_KEYS for wd, dl in DIL_PAIRS)
    assert seq % (DIL_KEYS * DIL_PAIRS[-1][1]) == 0 and seq % min(ATT_TILE, seq) == 0

    qkv_a = 3 * DA_COLS
    qkv_b = 3 * SB_COLS
    c0 = qkv_a + qkv_b
    cw = DIL_GROUPS * DIL_COLS

    def dil_cols(g):
        return [w_in[:, :, c0 + t * cw + g * DIL_COLS: c0 + t * cw + (g + 1) * DIL_COLS] for t in range(3)]

    w_main = jnp.concatenate([w_in[:, :, :2 * DA_COLS], w_in[:, :, qkv_a:qkv_a + 2 * SB_COLS]] + dil_cols(0),
                             axis=-1).astype(BF16)
    w_vt = jnp.concatenate([w_in[:, :, 2 * DA_COLS:qkv_a], w_in[:, :, qkv_a + 2 * SB_COLS:c0]],
                           axis=-1).transpose(0, 2, 1).astype(BF16)
    main_c0 = 2 * DA_COLS + 2 * SB_COLS
    ones = lambda width: jnp.ones((width,), F32)
    qscale = lambda width, dh: jnp.full((width,), LOG2E * dh ** -0.5, F32)
    cs_main = jnp.concatenate([qscale(DA_COLS, DA_QK_DIM), ones(DA_COLS), qscale(SB_COLS, SB_DIM), ones(SB_COLS),
                               qscale(DIL_COLS, DIL_DIM), ones(2 * DIL_COLS)]).reshape(1, -1)
    cs_dil = jnp.concatenate([qscale(DIL_COLS, DIL_DIM), ones(2 * DIL_COLS)]).reshape(1, -1)
    w_g1 = jnp.concatenate(dil_cols(1), axis=-1).astype(BF16)
    w_g2 = jnp.concatenate(dil_cols(2), axis=-1).astype(BF16)
    w_gate = w_in[:, :, c0 + 3 * cw:].astype(BF16)
    wa, wb, wc, wo = (w.astype(BF16) for w in (w_branch_da, w_branch_sb, w_branch_dil, w_out))
    wu, wd, wpg, wp = (w.astype(BF16) for w in (w_up, w_down, w_ple_gate, w_ple))

    dmax = max(seq, REL_MAX_DIST + 1)
    buckets = _rel_bucket(jnp.arange(dmax, dtype=jnp.int32))
    thr = jnp.sum(buckets[None, :] < jnp.arange(REL_BUCKETS, dtype=jnp.int32)[:, None], axis=1).astype(jnp.int32)
    tile = min(ATT_TILE, seq)
    da_bias = _da_bias_tiles(thr, rel_bias, seq, tile, min(DA_TQ, seq))
    dil_bias = _dil_bias_tiles(thr, rel_bias)

    xb = x.astype(BF16)
    for i in range(depth):
        lam_init = 0.8 - 0.6 * math.exp(-0.3 * i)
        proj = _proj(xb, w_main[i], cs_main, w_main.shape[-1] // 2)
        vt = _proj_t(xb, w_vt[i], tile)
        proj_g1 = _proj_dil(xb, w_g1[i], cs_dil, DIL_PAIRS[1][1])
        proj_g2 = _proj_dil(xb, w_g2[i], cs_dil, DIL_PAIRS[2][1])
        oa = _da_attention(proj, vt, da_bias, da_lambda[i], da_norm[i].reshape(DA_V_DIM, 1), lam_init)
        ob = _sb_attention(proj, vt)
        cgrp0 = _dil_attention(proj.reshape(bsz, 1, seq, proj.shape[-1]), main_c0 // DIL_COLS, dil_bias, 0)
        cgrp1 = _dil_attention(proj_g1, 0, dil_bias, 1)
        cgrp2 = _dil_attention(proj_g2, 0, dil_bias, 2)
        x1, x1b = _merge(x, xb, oa, ob, cgrp0, cgrp1, cgrp2, wa[i], wb[i], wc[i], w_gate[i], wo[i],
                         ln1_g[i].reshape(1, d), ln1_b[i].reshape(1, d), alpha)
        x2, x2b = _mlp(x1.reshape(n, d), x1b.reshape(n, d), p[i].reshape(n, PLE_DIM), wu[i], wd[i], wpg[i], wp[i],
                       ln2_g[i].reshape(1, d), ln2_b[i].reshape(1, d), alpha)
        x = x2.reshape(bsz, seq, d)
        xb = x2b.reshape(bsz, seq, d)
    return x
```

```python
import functools
import math

import jax
import jax.numpy as jnp
from jax import lax
from jax.experimental import pallas as pl
from jax.experimental.pallas import tpu as pltpu

D_MODEL = 1024
DA_HEADS = 4
DA_QK_DIM = 64
DA_V_DIM = 2 * DA_QK_DIM
SB_HEADS = 8
SB_DIM = 64
DIL_PAIRS = ((128, 1), (512, 4), (2048, 16))
DIL_GROUPS = len(DIL_PAIRS)
DIL_HEADS_PER_GROUP = 4
DIL_DIM = 128
DIL_KEYS = 128
D_FF = 4 * D_MODEL
PLE_DIM = 256
REL_BUCKETS = 32
REL_MAX_DIST = 2048
BIAS_HEADS = DA_HEADS + DIL_GROUPS * DIL_HEADS_PER_GROUP
LN_EPS = 1e-5
RMS_EPS = 1e-5

DA_COLS = DA_HEADS * 2 * DA_QK_DIM
SB_COLS = SB_HEADS * SB_DIM
DIL_COLS = DIL_HEADS_PER_GROUP * DIL_DIM
GATE_COLS = 3 * D_MODEL

LANES = 128
VMEM_LIMIT = 48 * 1024 * 1024

ATT_TILE = 256
DA_TQ = 2 * ATT_TILE
PROJ_TM = 1024
MERGE_TM = 256
MLP_TM = 512
MLP_TF = 1024
DIL_CHUNK = 1024

F32 = jnp.float32
BF16 = jnp.bfloat16
NEG_INF = float("-inf")
SIGN_BIT = -2 ** 31
SB_EXIT_BITS = 160.0
LOG2E = math.log2(math.e)


def _cparams(sem):
    return pltpu.CompilerParams(dimension_semantics=("arbitrary",) * len(sem), vmem_limit_bytes=VMEM_LIMIT)


def _dot(a, b):
    return jnp.dot(a, b, preferred_element_type=F32)


def _dot_nt(a, b):
    return lax.dot_general(a, b, (((1,), (1,)), ((), ())), preferred_element_type=F32)


def _sigmoid(v):
    return 1.0 / (1.0 + jnp.exp(-v))


def _layer_norm(y, g, b):
    mu = jnp.mean(y, axis=-1, keepdims=True)
    yc = y - mu
    var = jnp.mean(yc * yc, axis=-1, keepdims=True)
    return yc * lax.rsqrt(var + LN_EPS) * g + b


def _rel_bucket(dist):
    max_exact = REL_BUCKETS // 2
    d = jnp.maximum(dist, 0)
    log_ratio = jnp.log(jnp.maximum(d, 1).astype(F32) / max_exact) / math.log(REL_MAX_DIST / max_exact)
    large = max_exact + (log_ratio * (REL_BUCKETS - max_exact)).astype(jnp.int32)
    return jnp.where(d < max_exact, d, jnp.minimum(large, REL_BUCKETS - 1))


def _bias_lookup(d, head, thr_ref, tab_ref):
    val = jnp.full(d.shape, tab_ref[0, head], F32)
    for k in range(1, REL_BUCKETS):
        val = jnp.where(d >= thr_ref[k], tab_ref[k, head], val)
    return val


def _da_bias_kernel(thr_ref, tab_ref, o_ref, *, tk, tq):
    h = pl.program_id(0)
    delta = pl.program_id(1) - 1
    key = lax.broadcasted_iota(jnp.int32, (tk, tq), 0)
    query = lax.broadcasted_iota(jnp.int32, (tk, tq), 1)
    d = delta * tk + query - key
    val = _bias_lookup(d, h, thr_ref, tab_ref) * LOG2E
    o_ref[0, 0] = jnp.where(d >= 0, val, NEG_INF)


def _da_bias_tiles(thr, rel_bias, seq, tk, tq):
    assert tq == 2 * tk
    nd = 2 * (seq // tq)
    return pl.pallas_call(
        functools.partial(_da_bias_kernel, tk=tk, tq=tq),
        grid=(DA_HEADS, nd),
        in_specs=[pl.BlockSpec(memory_space=pltpu.SMEM), pl.BlockSpec(memory_space=pltpu.SMEM)],
        out_specs=pl.BlockSpec((1, 1, tk, tq), lambda h, d: (h, d, 0, 0)),
        out_shape=jax.ShapeDtypeStruct((DA_HEADS, nd, tk, tq), F32),
        compiler_params=_cparams(("arbitrary", "arbitrary")),
        name="da_bias_tiles",
    )(thr, rel_bias)


def _dil_bias_kernel(thr_ref, tab_ref, o_ref):
    g = pl.program_id(0)
    hs = pl.program_id(1)
    n = DIL_KEYS
    dil = jnp.where(g == 0, DIL_PAIRS[0][1], jnp.where(g == 1, DIL_PAIRS[1][1], DIL_PAIRS[2][1]))
    row = lax.broadcasted_iota(jnp.int32, (n, 2 * n), 0)
    col = lax.broadcasted_iota(jnp.int32, (n, 2 * n), 1)
    step = row + n - col
    head = DA_HEADS + g * DIL_HEADS_PER_GROUP + hs
    val = _bias_lookup(step * dil, head, thr_ref, tab_ref) * LOG2E
    valid = jnp.where(step >= 0, jnp.where(step <= n, 1, 0), 0)
    o_ref[0, 0, 0] = jnp.where(valid > 0, val, NEG_INF)
    o_ref[0, 0, 1] = jnp.where(jnp.where(col >= n, valid, 0) > 0, val, NEG_INF)


def _dil_bias_tiles(thr, rel_bias):
    n = DIL_KEYS
    return pl.pallas_call(
        _dil_bias_kernel,
        grid=(DIL_GROUPS, DIL_HEADS_PER_GROUP),
        in_specs=[pl.BlockSpec(memory_space=pltpu.SMEM), pl.BlockSpec(memory_space=pltpu.SMEM)],
        out_specs=pl.BlockSpec((1, 1, 2, n, 2 * n), lambda g, h: (g, h, 0, 0, 0)),
        out_shape=jax.ShapeDtypeStruct((DIL_GROUPS, DIL_HEADS_PER_GROUP, 2, n, 2 * n), F32),
        compiler_params=_cparams(("arbitrary", "arbitrary")),
        name="dil_bias_tiles",
    )(thr, rel_bias)


def _proj_kernel(x_ref, w_ref, cs_ref, o_ref):
    o_ref[0] = (_dot(x_ref[0], w_ref[...]) * cs_ref[...]).astype(o_ref.dtype)


def _proj(xb, w, col_scale, tn):
    bsz, seq, d = xb.shape
    c = w.shape[1]
    tm = min(PROJ_TM, seq)
    return pl.pallas_call(
        _proj_kernel,
        grid=(bsz, seq // tm, c // tn),
        in_specs=[pl.BlockSpec((1, tm, d), lambda b, i, j: (b, i, 0)),
                  pl.BlockSpec((d, tn), lambda b, i, j: (0, j)),
                  pl.BlockSpec((1, tn), lambda b, i, j: (0, j))],
        out_specs=pl.BlockSpec((1, tm, tn), lambda b, i, j: (b, i, j)),
        out_shape=jax.ShapeDtypeStruct((bsz, seq, c), BF16),
        compiler_params=_cparams(("parallel", "parallel", "arbitrary")),
        name="proj",
    )(xb, w, col_scale)


def _proj_t_kernel(x_ref, wt_ref, o_ref, *, tile):
    res = _dot_nt(wt_ref[...], x_ref[0]).astype(o_ref.dtype)
    for c in range(o_ref.shape[1]):
        o_ref[0, c] = res[:, c * tile:(c + 1) * tile]


def _proj_t(xb, wt, tile):
    bsz, seq, d = xb.shape
    c = wt.shape[0]
    tm = min(PROJ_TM, seq)
    return pl.pallas_call(
        functools.partial(_proj_t_kernel, tile=tile),
        grid=(bsz, seq // tm),
        in_specs=[pl.BlockSpec((1, tm, d), lambda b, i: (b, i, 0)),
                  pl.BlockSpec((c, d), lambda b, i: (0, 0))],
        out_specs=pl.BlockSpec((1, tm // tile, c, tile), lambda b, i: (b, i, 0, 0)),
        out_shape=jax.ShapeDtypeStruct((bsz, seq // tile, c, tile), BF16),
        compiler_params=_cparams(("parallel", "arbitrary")),
        name="proj_t",
    )(xb, wt)


def _proj_dil_kernel(x_ref, w_ref, cs_ref, o_ref, res_ref, *, dil, rows):
    res = _dot(x_ref[0], w_ref[...]) * cs_ref[...]
    nblk = res.shape[1] // LANES
    for c in range(nblk):
        res_ref[c] = res[:, c * LANES:(c + 1) * LANES]
    for r in range(dil):
        for c in range(nblk):
            o_ref[0, r, :, c * LANES:(c + 1) * LANES] = (
                res_ref[c, pl.ds(r, rows, stride=dil), :].astype(o_ref.dtype))


def _proj_dil(xb, w, col_scale, dil):
    bsz, seq, d = xb.shape
    c = w.shape[1]
    tm = min(PROJ_TM, seq)
    rows = tm // dil
    return pl.pallas_call(
        functools.partial(_proj_dil_kernel, dil=dil, rows=rows),
        grid=(bsz, seq // tm),
        in_specs=[pl.BlockSpec((1, tm, d), lambda b, i: (b, i, 0)),
                  pl.BlockSpec((d, c), lambda b, i: (0, 0)),
                  pl.BlockSpec((1, c), lambda b, i: (0, 0))],
        out_specs=pl.BlockSpec((1, dil, rows, c), lambda b, i: (b, 0, i, 0)),
        out_shape=jax.ShapeDtypeStruct((bsz, dil, seq // dil, c), BF16),
        scratch_shapes=[pltpu.VMEM((c // LANES, tm, LANES), F32)],
        compiler_params=_cparams(("parallel", "arbitrary")),
        name="proj_dil",
    )(xb, w, col_scale)


def _col_reduce(x, op):
    parts = [x[i:i + 8] for i in range(0, x.shape[0], 8)]
    while len(parts) > 1:
        parts = [op(parts[i], parts[i + 1]) for i in range(0, len(parts), 2)]
    return parts[0]


def _col_max(x):
    return jnp.max(_col_reduce(x, jnp.maximum), axis=0, keepdims=True)


def _col_sum(x):
    return jnp.sum(_col_reduce(x, jnp.add), axis=0, keepdims=True)


def _da_kernel(q_ref, k_ref, vt_ref, bias_ref, lam_ref, g_ref, o_ref, sa_ref, sb_ref, acc_ref, *, tk, tq, lam_init):
    qi = pl.program_id(2)
    nk = k_ref.shape[1] // tk
    lane = lax.broadcasted_iota(jnp.int32, (tq, LANES), 1)
    qs = q_ref[0]
    zero = jnp.zeros_like(qs)
    q_maps = (jnp.where(lane < DA_QK_DIM, qs, zero), jnp.where(lane >= DA_QK_DIM, qs, zero))

    def put_scores(ki, dst):
        start = pl.multiple_of(ki * tk, tk)
        k = k_ref[0, pl.ds(start, tk), :]
        for mp in range(2):
            dst[mp] = _dot_nt(k, q_maps[mp])

    col_blocks = [slice(c * LANES, (c + 1) * LANES) for c in range(tq // LANES)]
    ones_rows = jnp.ones((16, tk), BF16)

    def add_bias(ki, buf):
        bidx = jnp.maximum(2 * qi - ki + 1, 0)
        tile_max = []
        for mp in range(2):
            parts = []
            for cols in col_blocks:
                s = buf[mp, :, cols] + bias_ref[0, bidx, :, cols]
                buf[mp, :, cols] = s
                parts.append(_col_max(s))
            tile_max.append(jnp.concatenate(parts, axis=1))
        return tuple(tile_max)

    def exp_pv(ki, buf, tile_max, stats):
        vt = jnp.concatenate([vt_ref[0, ki], ones_rows], axis=0)
        new_stats, pending = [], []
        for mp in range(2):
            m, l = stats[mp]
            m_new = jnp.maximum(m, tile_max[mp])
            alpha = jnp.exp2(m - m_new)
            p = jnp.concatenate([jnp.exp2((buf[mp, :, cols] - m_new[:, cols]).astype(BF16)) for cols in col_blocks],
                                axis=1)
            pv = _dot(vt, p)
            new_stats.append((m_new, alpha * l + pv[DA_V_DIM:DA_V_DIM + 1, :]))
            pending.append((alpha, pv[:DA_V_DIM]))
        return tuple(new_stats), pending

    def accumulate(pending):
        for mp in range(2):
            alpha, pv = pending[mp]
            acc_ref[mp] = alpha * acc_ref[mp] + pv

    def pair(j, carry):
        stats, max_a = carry
        ka = 2 * j
        put_scores(ka + 1, sb_ref)
        stats, pend = exp_pv(ka, sa_ref, max_a, stats)
        max_b = add_bias(ka + 1, sb_ref)
        accumulate(pend)
        nxt = jnp.minimum(ka + 2, nk - 1)
        put_scores(nxt, sa_ref)
        stats, pend = exp_pv(ka + 1, sb_ref, max_b, stats)
        max_a = add_bias(nxt, sa_ref)
        accumulate(pend)
        return stats, max_a

    put_scores(0, sa_ref)
    acc_ref[...] = jnp.zeros_like(acc_ref)
    init = tuple((jnp.full((1, tq), NEG_INF, F32), jnp.zeros((1, tq), F32)) for _ in range(2))
    ((_, l1), (_, l2)), _ = lax.fori_loop(0, qi + 1, pair, (init, add_bias(0, sa_ref)))

    ll = lam_ref[...]
    lam = (jnp.exp(jnp.sum(ll[0:1] * ll[1:2], axis=-1, keepdims=True))
           - jnp.exp(jnp.sum(ll[2:3] * ll[3:4], axis=-1, keepdims=True)) + lam_init)
    o = acc_ref[0] * (1.0 / l1) - lam * (acc_ref[1] * (1.0 / l2))
    o = o * lax.rsqrt(jnp.mean(o * o, axis=0, keepdims=True) + RMS_EPS) * g_ref[...]
    o_ref[0] = (o * (1.0 - lam_init)).T.astype(o_ref.dtype)


def _da_attention(proj, vt, bias, lam_p, norm_g, lam_init):
    bsz, seq, _ = proj.shape
    tk, tq = bias.shape[-2:]
    nk = seq // tk
    kb = DA_COLS // LANES
    return pl.pallas_call(
        functools.partial(_da_kernel, tk=tk, tq=tq, lam_init=lam_init),
        grid=(DA_HEADS, bsz, seq // tq),
        in_specs=[pl.BlockSpec((1, tq, LANES), lambda h, b, i: (b, i, h)),
                  pl.BlockSpec((1, seq, LANES), lambda h, b, i: (b, 0, kb + h)),
                  pl.BlockSpec((1, nk, DA_V_DIM, tk), lambda h, b, i: (b, 0, h, 0)),
                  pl.BlockSpec((1, bias.shape[1], tk, tq), lambda h, b, i: (h, 0, 0, 0)),
                  pl.BlockSpec((4, DA_QK_DIM), lambda h, b, i: (0, 0)),
                  pl.BlockSpec((DA_V_DIM, 1), lambda h, b, i: (0, 0))],
        out_specs=pl.BlockSpec((1, tq, LANES), lambda h, b, i: (b, i, h)),
        out_shape=jax.ShapeDtypeStruct((bsz, seq, DA_COLS), BF16),
        scratch_shapes=[pltpu.VMEM((2, tk, tq), F32), pltpu.VMEM((2, tk, tq), F32),
                        pltpu.VMEM((2, DA_V_DIM, tq), F32)],
        compiler_params=_cparams(("arbitrary", "parallel", "arbitrary")),
        name="diff_attention",
    )(proj, proj, vt, bias, lam_p, norm_g)


def _sb_kernel(q_ref, k_ref, vt_ref, o_ref, za_ref, zb_ref, acc_ref, *, tk, tq):
    qi = pl.program_id(2)
    lane = lax.broadcasted_iota(jnp.int32, (tq, LANES), 1)
    qs = q_ref[0]
    zero = jnp.zeros_like(qs)
    q_heads = (jnp.where(lane < SB_DIM, qs, zero), jnp.where(lane >= SB_DIM, qs, zero))
    chan = lax.broadcasted_iota(jnp.int32, (LANES, tk), 0)
    head_chans = (chan < SB_DIM, chan >= SB_DIM)
    tri = jnp.where(lax.broadcasted_iota(jnp.int32, (tk, tk), 0) < lax.broadcasted_iota(jnp.int32, (tk, tk), 1),
                    1.0, 0.0).astype(BF16)
    key = lax.broadcasted_iota(jnp.int32, (tk, tq), 0)
    query = lax.broadcasted_iota(jnp.int32, (tk, tq), 1)

    def put_scores(ki, dst):
        start = pl.multiple_of(ki * tk, tk)
        k = k_ref[0, pl.ds(start, tk), :]
        for h in range(2):
            dst[h] = _dot_nt(k, q_heads[h])

    def process(ki, src, cs, before):
        vt = vt_ref[0, ki]
        later, own = [], []
        for h in range(2):
            z = src[h]
            neg_abs = pltpu.bitcast(pltpu.bitcast(z, jnp.int32) | SIGN_BIT, F32)
            drop = jnp.maximum(z, 0.0) + jnp.log2(1.0 + jnp.exp2(neg_abs))
            src[h] = z - drop
            if before is not None:
                drop = jnp.where(before, drop, 0.0)
            drop = drop.astype(BF16)
            later.append(_dot(tri, drop) + cs[h])
            own.append(drop[0:1, :].astype(F32))
        new_cs = []
        part = None
        for h in range(2):
            a = jnp.exp2(src[h] - later[h])
            if before is not None:
                a = jnp.where(before, a, 0.0)
            new_cs.append(later[h][0:1, :] + own[h])
            pv = _dot(jnp.where(head_chans[h], vt, jnp.zeros_like(vt)), a.astype(BF16))
            part = pv if part is None else part + pv
        return tuple(new_cs), part

    kd = 2 * qi + 1
    put_scores(kd, za_ref)
    put_scores(kd - 1, zb_ref)
    cs = (jnp.zeros((1, tq), F32), jnp.zeros((1, tq), F32))
    cs, part_a = process(kd, za_ref, cs, key + tk < query)
    put_scores(jnp.maximum(kd - 2, 0), za_ref)
    cs, part_b = process(kd - 1, zb_ref, cs, key < query)
    acc_ref[...] = part_a + part_b

    def least(cs):
        return jnp.min(jnp.minimum(cs[0], cs[1]))

    def pair(state):
        it, cs, _ = state
        ka = 2 * qi - 1 - 2 * it
        put_scores(ka - 1, zb_ref)
        cs, part_a = process(ka, za_ref, cs, None)
        acc_ref[...] += part_a

        def second_tile(cs):
            put_scores(jnp.maximum(ka - 2, 0), za_ref)
            cs, part_b = process(ka - 1, zb_ref, cs, None)
            acc_ref[...] += part_b
            return cs

        cs = lax.cond(least(cs) < SB_EXIT_BITS, second_tile, lambda cs: cs, cs)
        return it + 1, cs, least(cs)

    def more(state):
        it, _, cs_min = state
        return jnp.logical_and(it < qi, cs_min < SB_EXIT_BITS)

    lax.while_loop(more, pair, (jnp.int32(0), cs, least(cs)))
    o_ref[0] = acc_ref[...].T.astype(o_ref.dtype)


def _sb_attention(proj, vt):
    bsz, seq, _ = proj.shape
    tk = vt.shape[-1]
    tq = min(2 * tk, seq)
    nk = seq // tk
    qb = 2 * DA_COLS // LANES
    kb = qb + SB_COLS // LANES
    vb = DA_COLS // LANES
    return pl.pallas_call(
        functools.partial(_sb_kernel, tk=tk, tq=tq),
        grid=(bsz, SB_COLS // LANES, seq // tq),
        in_specs=[pl.BlockSpec((1, tq, LANES), lambda b, h, i: (b, i, qb + h)),
                  pl.BlockSpec((1, seq, LANES), lambda b, h, i: (b, 0, kb + h)),
                  pl.BlockSpec((1, nk, LANES, tk), lambda b, h, i: (b, 0, vb + h, 0))],
        out_specs=pl.BlockSpec((1, tq, LANES), lambda b, h, i: (b, i, h)),
        out_shape=jax.ShapeDtypeStruct((bsz, seq, SB_COLS), BF16),
        scratch_shapes=[pltpu.VMEM((2, tk, tq), F32), pltpu.VMEM((2, tk, tq), F32), pltpu.VMEM((LANES, tq), F32)],
        compiler_params=_cparams(("parallel", "parallel", "arbitrary")),
        name="stick_breaking_attention",
    )(proj, proj, vt)


def _dil_kernel(q_ref, kc_ref, kp_ref, vc_ref, vp_ref, bias_ref, o_ref, lse_ref, *, nblk):
    cc = pl.program_id(2)
    n = DIL_KEYS
    first = jnp.where(cc == 0, 1, 0)
    heads = range(DIL_HEADS_PER_GROUP)
    hcols = [slice(hs * DIL_DIM, (hs + 1) * DIL_DIM) for hs in heads]
    ones = jnp.ones((n, DIL_DIM), BF16)

    def blocks(descs):
        work = [(d, hs) for d in descs for hs in heads]
        sp, sc = [], []
        for (rows, kprev, _, prev_rows, variant), hs in work:
            qb = q_ref[0, 0, rows, hcols[hs]]
            bias = bias_ref[0, hs, variant]
            sp.append(_dot_nt(qb, kprev[0, 0, prev_rows, hcols[hs]]) + bias[:, :n])
            sc.append(_dot_nt(qb, kc_ref[0, 0, rows, hcols[hs]]) + bias[:, n:])
        m = [jnp.max(jnp.maximum(sp[i], sc[i]), axis=-1, keepdims=True) for i in range(len(work))]
        outs = []
        for i, ((rows, _, vprev, prev_rows, _), hs) in enumerate(work):
            pp = jnp.exp2(sp[i] - m[i]).astype(BF16)
            pc = jnp.exp2(sc[i] - m[i]).astype(BF16)
            outs.append(_dot(pp, jnp.concatenate([vprev[0, 0, prev_rows, hcols[hs]], ones], axis=1))
                        + _dot(pc, jnp.concatenate([vc_ref[0, 0, rows, hcols[hs]], ones], axis=1)))
        for i, ((rows, _, _, _, _), hs) in enumerate(work):
            o, l = outs[i][:, :DIL_DIM], outs[i][:, DIL_DIM:]
            o_ref[0, 0, rows, hcols[hs]] = o * (1.0 / l)
            lse_ref[0, 0, rows, hcols[hs]] = m[i] + jnp.log2(l)

    def desc(j):
        rows = slice(j * n, (j + 1) * n)
        if j == 0:
            return rows, kp_ref, vp_ref, slice(None), first
        return rows, kc_ref, vc_ref, slice((j - 1) * n, j * n), 0

    for j in range(0, nblk, 2):
        blocks([desc(jj) for jj in range(j, min(j + 2, nblk))])


def _dil_attention(qkv, col_blk, bias, group):
    bsz, dil, length, _ = qkv.shape
    n = DIL_KEYS
    chunk = min(DIL_CHUNK, length)
    nblk = chunk // n
    w = DIL_COLS
    cur = lambda off: pl.BlockSpec((1, 1, chunk, w), lambda b, r, c: (b, r, c, col_blk + off))
    prev = lambda off: pl.BlockSpec((1, 1, n, w),
                                    lambda b, r, c: (b, r, jnp.maximum(c * nblk - 1, 0), col_blk + off))
    out_spec = pl.BlockSpec((1, 1, chunk, w), lambda b, r, c: (b, r, c, 0))
    shape = jax.ShapeDtypeStruct((bsz, dil, length, w), F32)
    return pl.pallas_call(
        functools.partial(_dil_kernel, nblk=nblk),
        grid=(bsz, dil, length // chunk),
        in_specs=[cur(0), cur(1), prev(1), cur(2), prev(2),
                  pl.BlockSpec((1, DIL_HEADS_PER_GROUP, 2, n, 2 * n), lambda b, r, c: (group, 0, 0, 0, 0))],
        out_specs=[out_spec, out_spec],
        out_shape=[shape, shape],
        compiler_params=_cparams(("parallel", "parallel", "arbitrary")),
        name="dilated_attention",
    )(qkv, qkv, qkv, qkv, qkv, bias)


def _merge_kernel(x_ref, xb_ref, oa_ref, ob_ref, o0_ref, l0_ref, o1_ref, l1_ref, o2_ref, l2_ref,
                  wa_ref, wb_ref, wc_ref, wg_ref, wo_ref, g_ref, b_ref,
                  y_ref, yb_ref, s1o, s1l, s2o, s2l, oc_ref, *, alpha, tm):
    d = D_MODEL

    def combine_head_slot(c):
        cols = slice(c * DIL_DIM, (c + 1) * DIL_DIM)
        for src, dst in ((o1_ref, s1o), (l1_ref, s1l), (o2_ref, s2o), (l2_ref, s2l)):
            dil = src.shape[1]
            rows = tm // dil
            for r in range(dil):
                dst[c, pl.ds(r, rows, stride=dil), :] = src[0, r, :, cols]
        l0 = l0_ref[0, 0, :, cols]
        l1 = s1l[c]
        l2 = s2l[c]
        mx = jnp.maximum(jnp.maximum(l0, l1), l2)
        w0 = jnp.exp2(l0 - mx)
        w1 = jnp.exp2(l1 - mx)
        w2 = jnp.exp2(l2 - mx)
        oc = (w0 * o0_ref[0, 0, :, cols] + w1 * s1o[c] + w2 * s2o[c]) * (1.0 / (w0 + w1 + w2))
        oc_ref[:, cols] = oc.astype(BF16)

    gate_a = _dot(xb_ref[...], wg_ref[:, 0:d])
    combine_head_slot(0)
    branch_a = _dot(oa_ref[...], wa_ref[...])
    combine_head_slot(1)
    gate_b = _dot(xb_ref[...], wg_ref[:, d:2 * d])
    combine_head_slot(2)
    branch_b = _dot(ob_ref[...], wb_ref[...])
    combine_head_slot(3)
    gate_c = _dot(xb_ref[...], wg_ref[:, 2 * d:3 * d])
    merged = _sigmoid(gate_a) * branch_a + _sigmoid(gate_b) * branch_b
    merged = merged + _sigmoid(gate_c) * _dot(oc_ref[...], wc_ref[...])
    mixed = _dot(merged.astype(BF16), wo_ref[...])
    y = _layer_norm(alpha * x_ref[...] + mixed, g_ref[...], b_ref[...])
    y_ref[...] = y
    yb_ref[...] = y.astype(BF16)


def _merge(x, xb, oa, ob, c0, c1, c2, wa, wb, wc, wg, wo, g, b, alpha):
    bsz, seq, d = x.shape
    tm = min(MERGE_TM, seq)
    row = lambda width: pl.BlockSpec((None, tm, width), lambda bb, i: (bb, i, 0))
    full = lambda arr: pl.BlockSpec(arr.shape, lambda bb, i: (0,) * arr.ndim)

    def strided(arr):
        dil = arr.shape[1]
        return pl.BlockSpec((1, dil, tm // dil, DIL_COLS), lambda bb, i: (bb, 0, i, 0))

    weights = (wa, wb, wc, wg, wo, g, b)
    return pl.pallas_call(
        functools.partial(_merge_kernel, alpha=alpha, tm=tm),
        grid=(bsz, seq // tm),
        in_specs=[row(d), row(d), row(DA_COLS), row(SB_COLS),
                  strided(c0[0]), strided(c0[1]), strided(c1[0]), strided(c1[1]),
                  strided(c2[0]), strided(c2[1])] + [full(a) for a in weights],
        out_specs=[row(d), row(d)],
        out_shape=[jax.ShapeDtypeStruct((bsz, seq, d), F32), jax.ShapeDtypeStruct((bsz, seq, d), BF16)],
        scratch_shapes=[pltpu.VMEM((DIL_HEADS_PER_GROUP, tm, DIL_DIM), F32) for _ in range(4)]
        + [pltpu.VMEM((tm, DIL_COLS), BF16)],
        compiler_params=_cparams(("parallel", "arbitrary")),
        name="merge_outproj_ln",
    )(x, xb, oa, ob, c0[0], c0[1], c1[0], c1[1], c2[0], c2[1], *weights)


def _mlp_kernel(x_ref, xb_ref, p_ref, wu_ref, wd_ref, wpg_ref, wp_ref, g_ref, b_ref, y_ref, yb_ref, *, alpha, tf):
    xb = xb_ref[...]
    mlp = None
    for f in range(D_FF // tf):
        hid = jnp.maximum(_dot(xb, wu_ref[:, f * tf:(f + 1) * tf]), 0.0)
        part = _dot((hid * hid).astype(BF16), wd_ref[f * tf:(f + 1) * tf, :])
        mlp = part if mlp is None else mlp + part
    ple = _sigmoid(_dot(xb, wpg_ref[...])) * _dot(p_ref[...].astype(BF16), wp_ref[...])
    y = _layer_norm(alpha * x_ref[...] + mlp + ple, g_ref[...], b_ref[...])
    y_ref[...] = y
    yb_ref[...] = y.astype(BF16)


def _mlp(x, xb, p, wu, wd, wpg, wp, g, b, alpha):
    n, d = x.shape
    tm = min(MLP_TM, n)
    row = lambda width: pl.BlockSpec((tm, width), lambda i: (i, 0))
    full = lambda arr: pl.BlockSpec(arr.shape, lambda i: (0,) * arr.ndim, pipeline_mode=pl.Buffered(1))
    return pl.pallas_call(
        functools.partial(_mlp_kernel, alpha=alpha, tf=MLP_TF),
        grid=(n // tm,),
        in_specs=[row(d), row(d), row(PLE_DIM), full(wu), full(wd), full(wpg), full(wp), full(g), full(b)],
        out_specs=[row(d), row(d)],
        out_shape=[jax.ShapeDtypeStruct((n, d), F32), jax.ShapeDtypeStruct((n, d), BF16)],
        compiler_params=_cparams(("parallel",)),
        name="mlp_ple_ln",
    )(x, xb, p, wu, wd, wpg, wp, g, b)


def kernel(x, p, w_in, da_lambda, da_norm, w_branch_da, w_branch_sb, w_branch_dil, w_out,
           ln1_g, ln1_b, w_up, w_down, w_ple_gate, w_ple, ln2_g, ln2_b, rel_bias):
    bsz, seq, d = x.shape
    depth = w_in.shape[0]
    n = bsz * seq
    alpha = (2 * depth) ** 0.25
    assert d == D_MODEL and all(wd // dl == DIL_KEYS for wd, dl in DIL_PAIRS)
    assert seq % (DIL_KEYS * DIL_PAIRS[-1][1]) == 0 and seq % min(ATT_TILE, seq) == 0

    qkv_a = 3 * DA_COLS
    qkv_b = 3 * SB_COLS
    c0 = qkv_a + qkv_b
    cw = DIL_GROUPS * DIL_COLS

    def dil_cols(g):
        return [w_in[:, :, c0 + t * cw + g * DIL_COLS: c0 + t * cw + (g + 1) * DIL_COLS] for t in range(3)]

    w_main = jnp.concatenate([w_in[:, :, :2 * DA_COLS], w_in[:, :, qkv_a:qkv_a + 2 * SB_COLS]] + dil_cols(0),
                             axis=-1).astype(BF16)
    w_vt = jnp.concatenate([w_in[:, :, 2 * DA_COLS:qkv_a], w_in[:, :, qkv_a + 2 * SB_COLS:c0]],
                           axis=-1).transpose(0, 2, 1).astype(BF16)
    main_c0 = 2 * DA_COLS + 2 * SB_COLS
    ones = lambda width: jnp.ones((width,), F32)
    qscale = lambda width, dh: jnp.full((width,), LOG2E * dh ** -0.5, F32)
    cs_main = jnp.concatenate([qscale(DA_COLS, DA_QK_DIM), ones(DA_COLS), qscale(SB_COLS, SB_DIM), ones(SB_COLS),
                               qscale(DIL_COLS, DIL_DIM), ones(2 * DIL_COLS)]).reshape(1, -1)
    cs_dil = jnp.concatenate([qscale(DIL_COLS, DIL_DIM), ones(2 * DIL_COLS)]).reshape(1, -1)
    w_g1 = jnp.concatenate(dil_cols(1), axis=-1).astype(BF16)
    w_g2 = jnp.concatenate(dil_cols(2), axis=-1).astype(BF16)
    w_gate = w_in[:, :, c0 + 3 * cw:].astype(BF16)
    wa, wb, wc, wo = (w.astype(BF16) for w in (w_branch_da, w_branch_sb, w_branch_dil, w_out))
    wu, wd, wpg, wp = (w.astype(BF16) for w in (w_up, w_down, w_ple_gate, w_ple))

    dmax = max(seq, REL_MAX_DIST + 1)
    buckets = _rel_bucket(jnp.arange(dmax, dtype=jnp.int32))
    thr = jnp.sum(buckets[None, :] < jnp.arange(REL_BUCKETS, dtype=jnp.int32)[:, None], axis=1).astype(jnp.int32)
    tile = min(ATT_TILE, seq)
    da_bias = _da_bias_tiles(thr, rel_bias, seq, tile, min(DA_TQ, seq))
    dil_bias = _dil_bias_tiles(thr, rel_bias)

    xb = x.astype(BF16)
    for i in range(depth):
        lam_init = 0.8 - 0.6 * math.exp(-0.3 * i)
        proj = _proj(xb, w_main[i], cs_main, w_main.shape[-1] // 2)
        vt = _proj_t(xb, w_vt[i], tile)
        proj_g1 = _proj_dil(xb, w_g1[i], cs_dil, DIL_PAIRS[1][1])
        proj_g2 = _proj_dil(xb, w_g2[i], cs_dil, DIL_PAIRS[2][1])
        oa = _da_attention(proj, vt, da_bias, da_lambda[i], da_norm[i].reshape(DA_V_DIM, 1), lam_init)
        ob = _sb_attention(proj, vt)
        cgrp0 = _dil_attention(proj.reshape(bsz, 1, seq, proj.shape[-1]), main_c0 // DIL_COLS, dil_bias, 0)
        cgrp1 = _dil_attention(proj_g1, 0, dil_bias, 1)
        cgrp2 = _dil_attention(proj_g2, 0, dil_bias, 2)
        x1, x1b = _merge(x, xb, oa, ob, cgrp0, cgrp1, cgrp2, wa[i], wb[i], wc[i], w_gate[i], wo[i],
                         ln1_g[i].reshape(1, d), ln1_b[i].reshape(1, d), alpha)
        x2, x2b = _mlp(x1.reshape(n, d), x1b.reshape(n, d), p[i].reshape(n, PLE_DIM), wu[i], wd[i], wpg[i], wp[i],
                       ln2_g[i].reshape(1, d), ln2_b[i].reshape(1, d), alpha)
        x = x2.reshape(bsz, seq, d)
        xb = x2b.reshape(bsz, seq, d)
    return x
```

```python
import functools
import math

import jax
import jax.numpy as jnp
from jax import lax
from jax.experimental import pallas as pl
from jax.experimental.pallas import tpu as pltpu

D_MODEL = 1024
DA_HEADS = 4
DA_QK_DIM = 64
DA_V_DIM = 2 * DA_QK_DIM
SB_HEADS = 8
SB_DIM = 64
DIL_PAIRS = ((128, 1), (512, 4), (2048, 16))
DIL_GROUPS = len(DIL_PAIRS)
DIL_HEADS_PER_GROUP = 4
DIL_DIM = 128
DIL_KEYS = 128
D_FF = 4 * D_MODEL
PLE_DIM = 256
REL_BUCKETS = 32
REL_MAX_DIST = 2048
BIAS_HEADS = DA_HEADS + DIL_GROUPS * DIL_HEADS_PER_GROUP
LN_EPS = 1e-5
RMS_EPS = 1e-5

DA_COLS = DA_HEADS * 2 * DA_QK_DIM
SB_COLS = SB_HEADS * SB_DIM
DIL_COLS = DIL_HEADS_PER_GROUP * DIL_DIM
GATE_COLS = 3 * D_MODEL

LANES = 128
VMEM_LIMIT = 48 * 1024 * 1024

ATT_TILE = 256
DA_TQ = 2 * ATT_TILE
PROJ_TM = 1024
MERGE_TM = 256
MLP_TM = 512
MLP_TF = 1024
DIL_CHUNK = 1024

F32 = jnp.float32
BF16 = jnp.bfloat16
NEG_INF = float("-inf")
SIGN_BIT = -2 ** 31
SB_EXIT_BITS = 160.0
LOG2E = math.log2(math.e)


def _cparams(sem):
    return pltpu.CompilerParams(dimension_semantics=("arbitrary",) * len(sem), vmem_limit_bytes=VMEM_LIMIT)


def _dot(a, b):
    return jnp.dot(a, b, preferred_element_type=F32)


def _dot_nt(a, b):
    return lax.dot_general(a, b, (((1,), (1,)), ((), ())), preferred_element_type=F32)


def _sigmoid(v):
    return 1.0 / (1.0 + jnp.exp(-v))


def _layer_norm(y, g, b):
    mu = jnp.mean(y, axis=-1, keepdims=True)
    yc = y - mu
    var = jnp.mean(yc * yc, axis=-1, keepdims=True)
    return yc * lax.rsqrt(var + LN_EPS) * g + b


def _rel_bucket(dist):
    max_exact = REL_BUCKETS // 2
    d = jnp.maximum(dist, 0)
    log_ratio = jnp.log(jnp.maximum(d, 1).astype(F32) / max_exact) / math.log(REL_MAX_DIST / max_exact)
    large = max_exact + (log_ratio * (REL_BUCKETS - max_exact)).astype(jnp.int32)
    return jnp.where(d < max_exact, d, jnp.minimum(large, REL_BUCKETS - 1))


def _bias_lookup(d, head, thr_ref, tab_ref):
    val = jnp.full(d.shape, tab_ref[0, head], F32)
    for k in range(1, REL_BUCKETS):
        val = jnp.where(d >= thr_ref[k], tab_ref[k, head], val)
    return val


def _da_bias_kernel(thr_ref, tab_ref, o_ref, *, tk, tq):
    h = pl.program_id(0)
    delta = pl.program_id(1) - 1
    key = lax.broadcasted_iota(jnp.int32, (tk, tq), 0)
    query = lax.broadcasted_iota(jnp.int32, (tk, tq), 1)
    d = delta * tk + query - key
    val = _bias_lookup(d, h, thr_ref, tab_ref) * LOG2E
    o_ref[0, 0] = jnp.where(d >= 0, val, NEG_INF)


def _da_bias_tiles(thr, rel_bias, seq, tk, tq):
    assert tq == 2 * tk
    nd = 2 * (seq // tq)
    return pl.pallas_call(
        functools.partial(_da_bias_kernel, tk=tk, tq=tq),
        grid=(DA_HEADS, nd),
        in_specs=[pl.BlockSpec(memory_space=pltpu.SMEM), pl.BlockSpec(memory_space=pltpu.SMEM)],
        out_specs=pl.BlockSpec((1, 1, tk, tq), lambda h, d: (h, d, 0, 0)),
        out_shape=jax.ShapeDtypeStruct((DA_HEADS, nd, tk, tq), F32),
        compiler_params=_cparams(("arbitrary", "arbitrary")),
        name="da_bias_tiles",
    )(thr, rel_bias)


def _dil_bias_kernel(thr_ref, tab_ref, o_ref):
    g = pl.program_id(0)
    hs = pl.program_id(1)
    n = DIL_KEYS
    dil = jnp.where(g == 0, DIL_PAIRS[0][1], jnp.where(g == 1, DIL_PAIRS[1][1], DIL_PAIRS[2][1]))
    row = lax.broadcasted_iota(jnp.int32, (n, 2 * n), 0)
    col = lax.broadcasted_iota(jnp.int32, (n, 2 * n), 1)
    step = row + n - col
    head = DA_HEADS + g * DIL_HEADS_PER_GROUP + hs
    val = _bias_lookup(step * dil, head, thr_ref, tab_ref) * LOG2E
    valid = jnp.where(step >= 0, jnp.where(step <= n, 1, 0), 0)
    o_ref[0, 0, 0] = jnp.where(valid > 0, val, NEG_INF)
    o_ref[0, 0, 1] = jnp.where(jnp.where(col >= n, valid, 0) > 0, val, NEG_INF)


def _dil_bias_tiles(thr, rel_bias):
    n = DIL_KEYS
    return pl.pallas_call(
        _dil_bias_kernel,
        grid=(DIL_GROUPS, DIL_HEADS_PER_GROUP),
        in_specs=[pl.BlockSpec(memory_space=pltpu.SMEM), pl.BlockSpec(memory_space=pltpu.SMEM)],
        out_specs=pl.BlockSpec((1, 1, 2, n, 2 * n), lambda g, h: (g, h, 0, 0, 0)),
        out_shape=jax.ShapeDtypeStruct((DIL_GROUPS, DIL_HEADS_PER_GROUP, 2, n, 2 * n), F32),
        compiler_params=_cparams(("arbitrary", "arbitrary")),
        name="dil_bias_tiles",
    )(thr, rel_bias)


def _proj_kernel(x_ref, w_ref, cs_ref, o_ref):
    o_ref[0] = (_dot(x_ref[0], w_ref[...]) * cs_ref[...]).astype(o_ref.dtype)


def _proj(xb, w, col_scale, tn):
    bsz, seq, d = xb.shape
    c = w.shape[1]
    tm = min(PROJ_TM, seq)
    return pl.pallas_call(
        _proj_kernel,
        grid=(bsz, seq // tm, c // tn),
        in_specs=[pl.BlockSpec((1, tm, d), lambda b, i, j: (b, i, 0)),
                  pl.BlockSpec((d, tn), lambda b, i, j: (0, j)),
                  pl.BlockSpec((1, tn), lambda b, i, j: (0, j))],
        out_specs=pl.BlockSpec((1, tm, tn), lambda b, i, j: (b, i, j)),
        out_shape=jax.ShapeDtypeStruct((bsz, seq, c), BF16),
        compiler_params=_cparams(("parallel", "parallel", "arbitrary")),
        name="proj",
    )(xb, w, col_scale)


def _proj_t_kernel(x_ref, wt_ref, o_ref, *, tile):
    res = _dot_nt(wt_ref[...], x_ref[0]).astype(o_ref.dtype)
    for c in range(o_ref.shape[1]):
        o_ref[0, c] = res[:, c * tile:(c + 1) * tile]


def _proj_t(xb, wt, tile):
    bsz, seq, d = xb.shape
    c = wt.shape[0]
    tm = min(PROJ_TM, seq)
    return pl.pallas_call(
        functools.partial(_proj_t_kernel, tile=tile),
        grid=(bsz, seq // tm),
        in_specs=[pl.BlockSpec((1, tm, d), lambda b, i: (b, i, 0)),
                  pl.BlockSpec((c, d), lambda b, i: (0, 0))],
        out_specs=pl.BlockSpec((1, tm // tile, c, tile), lambda b, i: (b, i, 0, 0)),
        out_shape=jax.ShapeDtypeStruct((bsz, seq // tile, c, tile), BF16),
        compiler_params=_cparams(("parallel", "arbitrary")),
        name="proj_t",
    )(xb, wt)


def _proj_dil_kernel(x_ref, w_ref, cs_ref, o_ref, res_ref, *, dil, rows):
    res = _dot(x_ref[0], w_ref[...]) * cs_ref[...]
    nblk = res.shape[1] // LANES
    for c in range(nblk):
        res_ref[c] = res[:, c * LANES:(c + 1) * LANES]
    for r in range(dil):
        for c in range(nblk):
            o_ref[0, r, :, c * LANES:(c + 1) * LANES] = (
                res_ref[c, pl.ds(r, rows, stride=dil), :].astype(o_ref.dtype))


def _proj_dil(xb, w, col_scale, dil):
    bsz, seq, d = xb.shape
    c = w.shape[1]
    tm = min(PROJ_TM, seq)
    rows = tm // dil
    return pl.pallas_call(
        functools.partial(_proj_dil_kernel, dil=dil, rows=rows),
        grid=(bsz, seq // tm),
        in_specs=[pl.BlockSpec((1, tm, d), lambda b, i: (b, i, 0)),
                  pl.BlockSpec((d, c), lambda b, i: (0, 0)),
                  pl.BlockSpec((1, c), lambda b, i: (0, 0))],
        out_specs=pl.BlockSpec((1, dil, rows, c), lambda b, i: (b, 0, i, 0)),
        out_shape=jax.ShapeDtypeStruct((bsz, dil, seq // dil, c), BF16),
        scratch_shapes=[pltpu.VMEM((c // LANES, tm, LANES), F32)],
        compiler_params=_cparams(("parallel", "arbitrary")),
        name="proj_dil",
    )(xb, w, col_scale)


def _col_reduce(x, op):
    parts = [x[i:i + 8] for i in range(0, x.shape[0], 8)]
    while len(parts) > 1:
        parts = [op(parts[i], parts[i + 1]) for i in range(0, len(parts), 2)]
    return parts[0]


def _col_max(x):
    return jnp.max(_col_reduce(x, jnp.maximum), axis=0, keepdims=True)


def _col_sum(x):
    return jnp.sum(_col_reduce(x, jnp.add), axis=0, keepdims=True)


def _da_kernel(q_ref, k_ref, vt_ref, bias_ref, lam_ref, g_ref, o_ref, sa_ref, sb_ref, acc_ref, *, tk, tq, lam_init):
    qi = pl.program_id(2)
    nk = k_ref.shape[1] // tk
    lane = lax.broadcasted_iota(jnp.int32, (tq, LANES), 1)
    qs = q_ref[0]
    zero = jnp.zeros_like(qs)
    q_maps = (jnp.where(lane < DA_QK_DIM, qs, zero), jnp.where(lane >= DA_QK_DIM, qs, zero))

    def put_scores(ki, dst):
        start = pl.multiple_of(ki * tk, tk)
        k = k_ref[0, pl.ds(start, tk), :]
        for mp in range(2):
            dst[mp] = _dot_nt(k, q_maps[mp])

    col_blocks = [slice(c * LANES, (c + 1) * LANES) for c in range(tq // LANES)]
    ones_rows = jnp.ones((16, tk), BF16)

    def add_bias(ki, buf):
        bidx = jnp.maximum(2 * qi - ki + 1, 0)
        tile_max = []
        for mp in range(2):
            parts = []
            for cols in col_blocks:
                s = buf[mp, :, cols] + bias_ref[0, bidx, :, cols]
                buf[mp, :, cols] = s
                parts.append(_col_max(s))
            tile_max.append(jnp.concatenate(parts, axis=1))
        return tuple(tile_max)

    def exp_pv(ki, buf, tile_max, stats):
        vt = jnp.concatenate([vt_ref[0, ki], ones_rows], axis=0)
        new_stats, pending = [], []
        for mp in range(2):
            m, l = stats[mp]
            m_new = jnp.maximum(m, tile_max[mp])
            alpha = jnp.exp2(m - m_new)
            p = jnp.concatenate([jnp.exp2((buf[mp, :, cols] - m_new[:, cols]).astype(BF16)) for cols in col_blocks],
                                axis=1)
            pv = _dot(vt, p)
            new_stats.append((m_new, alpha * l + pv[DA_V_DIM:DA_V_DIM + 1, :]))
            pending.append((alpha, pv[:DA_V_DIM]))
        return tuple(new_stats), pending

    def accumulate(pending):
        for mp in range(2):
            alpha, pv = pending[mp]
            acc_ref[mp] = alpha * acc_ref[mp] + pv

    def pair(j, carry):
        stats, max_a = carry
        ka = 2 * j
        put_scores(ka + 1, sb_ref)
        stats, pend = exp_pv(ka, sa_ref, max_a, stats)
        max_b = add_bias(ka + 1, sb_ref)
        accumulate(pend)
        nxt = jnp.minimum(ka + 2, nk - 1)
        put_scores(nxt, sa_ref)
        stats, pend = exp_pv(ka + 1, sb_ref, max_b, stats)
        max_a = add_bias(nxt, sa_ref)
        accumulate(pend)
        return stats, max_a

    put_scores(0, sa_ref)
    acc_ref[...] = jnp.zeros_like(acc_ref)
    init = tuple((jnp.full((1, tq), NEG_INF, F32), jnp.zeros((1, tq), F32)) for _ in range(2))
    ((_, l1), (_, l2)), _ = lax.fori_loop(0, qi + 1, pair, (init, add_bias(0, sa_ref)))

    ll = lam_ref[...]
    lam = (jnp.exp(jnp.sum(ll[0:1] * ll[1:2], axis=-1, keepdims=True))
           - jnp.exp(jnp.sum(ll[2:3] * ll[3:4], axis=-1, keepdims=True)) + lam_init)
    o = acc_ref[0] * (1.0 / l1) - lam * (acc_ref[1] * (1.0 / l2))
    o = o * lax.rsqrt(jnp.mean(o * o, axis=0, keepdims=True) + RMS_EPS) * g_ref[...]
    o_ref[0] = (o * (1.0 - lam_init)).T.astype(o_ref.dtype)


def _da_attention(proj, vt, bias, lam_p, norm_g, lam_init):
    bsz, seq, _ = proj.shape
    tk, tq = bias.shape[-2:]
    nk = seq // tk
    kb = DA_COLS // LANES
    return pl.pallas_call(
        functools.partial(_da_kernel, tk=tk, tq=tq, lam_init=lam_init),
        grid=(DA_HEADS, bsz, seq // tq),
        in_specs=[pl.BlockSpec((1, tq, LANES), lambda h, b, i: (b, i, h)),
                  pl.BlockSpec((1, seq, LANES), lambda h, b, i: (b, 0, kb + h)),
                  pl.BlockSpec((1, nk, DA_V_DIM, tk), lambda h, b, i: (b, 0, h, 0)),
                  pl.BlockSpec((1, bias.shape[1], tk, tq), lambda h, b, i: (h, 0, 0, 0)),
                  pl.BlockSpec((4, DA_QK_DIM), lambda h, b, i: (0, 0)),
                  pl.BlockSpec((DA_V_DIM, 1), lambda h, b, i: (0, 0))],
        out_specs=pl.BlockSpec((1, tq, LANES), lambda h, b, i: (b, i, h)),
        out_shape=jax.ShapeDtypeStruct((bsz, seq, DA_COLS), BF16),
        scratch_shapes=[pltpu.VMEM((2, tk, tq), F32), pltpu.VMEM((2, tk, tq), F32),
                        pltpu.VMEM((2, DA_V_DIM, tq), F32)],
        compiler_params=_cparams(("arbitrary", "parallel", "arbitrary")),
        name="diff_attention",
    )(proj, proj, vt, bias, lam_p, norm_g)


def _sb_kernel(q_ref, k_ref, vt_ref, o_ref, za_ref, zb_ref, acc_ref, *, tk, tq):
    qi = pl.program_id(2)
    lane = lax.broadcasted_iota(jnp.int32, (tq, LANES), 1)
    qs = q_ref[0]
    zero = jnp.zeros_like(qs)
    q_heads = (jnp.where(lane < SB_DIM, qs, zero), jnp.where(lane >= SB_DIM, qs, zero))
    chan = lax.broadcasted_iota(jnp.int32, (LANES, tk), 0)
    head_chans = (chan < SB_DIM, chan >= SB_DIM)
    tri = jnp.where(lax.broadcasted_iota(jnp.int32, (tk, tk), 0) < lax.broadcasted_iota(jnp.int32, (tk, tk), 1),
                    1.0, 0.0).astype(BF16)
    key = lax.broadcasted_iota(jnp.int32, (tk, tq), 0)
    query = lax.broadcasted_iota(jnp.int32, (tk, tq), 1)

    def put_scores(ki, dst):
        start = pl.multiple_of(ki * tk, tk)
        k = k_ref[0, pl.ds(start, tk), :]
        for h in range(2):
            dst[h] = _dot_nt(k, q_heads[h])

    def process(ki, src, cs, before):
        vt = vt_ref[0, ki]
        later, own = [], []
        for h in range(2):
            z = src[h]
            neg_abs = pltpu.bitcast(pltpu.bitcast(z, jnp.int32) | SIGN_BIT, F32)
            drop = jnp.maximum(z, 0.0) + jnp.log2(1.0 + jnp.exp2(neg_abs))
            src[h] = z - drop
            if before is not None:
                drop = jnp.where(before, drop, 0.0)
            drop = drop.astype(BF16)
            later.append(_dot(tri, drop) + cs[h])
            own.append(drop[0:1, :].astype(F32))
        new_cs = []
        part = None
        for h in range(2):
            a = jnp.exp2(src[h] - later[h])
            if before is not None:
                a = jnp.where(before, a, 0.0)
            new_cs.append(later[h][0:1, :] + own[h])
            pv = _dot(jnp.where(head_chans[h], vt, jnp.zeros_like(vt)), a.astype(BF16))
            part = pv if part is None else part + pv
        return tuple(new_cs), part

    kd = 2 * qi + 1
    put_scores(kd, za_ref)
    put_scores(kd - 1, zb_ref)
    cs = (jnp.zeros((1, tq), F32), jnp.zeros((1, tq), F32))
    cs, part_a = process(kd, za_ref, cs, key + tk < query)
    put_scores(jnp.maximum(kd - 2, 0), za_ref)
    cs, part_b = process(kd - 1, zb_ref, cs, key < query)
    acc_ref[...] = part_a + part_b

    def least(cs):
        return jnp.min(jnp.minimum(cs[0], cs[1]))

    def pair(state):
        it, cs, _ = state
        ka = 2 * qi - 1 - 2 * it
        put_scores(ka - 1, zb_ref)
        cs, part_a = process(ka, za_ref, cs, None)
        acc_ref[...] += part_a

        def second_tile(cs):
            put_scores(jnp.maximum(ka - 2, 0), za_ref)
            cs, part_b = process(ka - 1, zb_ref, cs, None)
            acc_ref[...] += part_b
            return cs

        cs = lax.cond(least(cs) < SB_EXIT_BITS, second_tile, lambda cs: cs, cs)
        return it + 1, cs, least(cs)

    def more(state):
        it, _, cs_min = state
        return jnp.logical_and(it < qi, cs_min < SB_EXIT_BITS)

    lax.while_loop(more, pair, (jnp.int32(0), cs, least(cs)))
    o_ref[0] = acc_ref[...].T.astype(o_ref.dtype)


def _sb_attention(proj, vt):
    bsz, seq, _ = proj.shape
    tk = vt.shape[-1]
    tq = min(2 * tk, seq)
    nk = seq // tk
    qb = 2 * DA_COLS // LANES
    kb = qb + SB_COLS // LANES
    vb = DA_COLS // LANES
    return pl.pallas_call(
        functools.partial(_sb_kernel, tk=tk, tq=tq),
        grid=(bsz, SB_COLS // LANES, seq // tq),
        in_specs=[pl.BlockSpec((1, tq, LANES), lambda b, h, i: (b, i, qb + h)),
                  pl.BlockSpec((1, seq, LANES), lambda b, h, i: (b, 0, kb + h)),
                  pl.BlockSpec((1, nk, LANES, tk), lambda b, h, i: (b, 0, vb + h, 0))],
        out_specs=pl.BlockSpec((1, tq, LANES), lambda b, h, i: (b, i, h)),
        out_shape=jax.ShapeDtypeStruct((bsz, seq, SB_COLS), BF16),
        scratch_shapes=[pltpu.VMEM((2, tk, tq), F32), pltpu.VMEM((2, tk, tq), F32), pltpu.VMEM((LANES, tq), F32)],
        compiler_params=_cparams(("parallel", "parallel", "arbitrary")),
        name="stick_breaking_attention",
    )(proj, proj, vt)


def _dil_kernel(q_ref, kc_ref, kp_ref, vc_ref, vp_ref, bias_ref, o_ref, lse_ref, *, nblk):
    cc = pl.program_id(2)
    n = DIL_KEYS
    first = jnp.where(cc == 0, 1, 0)
    heads = range(DIL_HEADS_PER_GROUP)
    hcols = [slice(hs * DIL_DIM, (hs + 1) * DIL_DIM) for hs in heads]
    ones = jnp.ones((n, DIL_DIM), BF16)

    def blocks(descs):
        work = [(d, hs) for d in descs for hs in heads]
        sp, sc = [], []
        for (rr, rows, kprev, _, prev_rows, variant), hs in work:
            qb = q_ref[0, rr, rows, hcols[hs]]
            bias = bias_ref[0, hs, variant]
            sp.append(_dot_nt(qb, kprev[0, rr, prev_rows, hcols[hs]]) + bias[:, :n])
            sc.append(_dot_nt(qb, kc_ref[0, rr, rows, hcols[hs]]) + bias[:, n:])
        m = [jnp.max(jnp.maximum(sp[i], sc[i]), axis=-1, keepdims=True) for i in range(len(work))]
        outs = []
        for i, ((rr, rows, _, vprev, prev_rows, _), hs) in enumerate(work):
            pp = jnp.exp2(sp[i] - m[i]).astype(BF16)
            pc = jnp.exp2(sc[i] - m[i]).astype(BF16)
            outs.append(_dot(pp, jnp.concatenate([vprev[0, rr, prev_rows, hcols[hs]], ones], axis=1))
                        + _dot(pc, jnp.concatenate([vc_ref[0, rr, rows, hcols[hs]], ones], axis=1)))
        for i, ((rr, rows, _, _, _, _), hs) in enumerate(work):
            o, l = outs[i][:, :DIL_DIM], outs[i][:, DIL_DIM:]
            o_ref[0, rr, rows, hcols[hs]] = o * (1.0 / l)
            lse_ref[0, rr, rows, hcols[hs]] = m[i] + jnp.log2(l)

    def desc(rr, j):
        rows = slice(j * n, (j + 1) * n)
        if j == 0:
            return rr, rows, kp_ref, vp_ref, slice(None), first
        return rr, rows, kc_ref, vc_ref, slice((j - 1) * n, j * n), 0

    descs = [desc(rr, j) for rr in range(q_ref.shape[1]) for j in range(nblk)]
    for g in range(0, len(descs), 2):
        blocks(descs[g:g + 2])


def _dil_attention(qkv, col_blk, bias, group):
    bsz, dil, length, _ = qkv.shape
    n = DIL_KEYS
    chunk = min(DIL_CHUNK, length)
    nblk = chunk // n
    w = DIL_COLS
    nres = max(1, min(dil, DIL_CHUNK // chunk))
    cur = lambda off: pl.BlockSpec((1, nres, chunk, w), lambda b, r, c: (b, r, c, col_blk + off))
    prev = lambda off: pl.BlockSpec((1, nres, n, w),
                                    lambda b, r, c: (b, r, jnp.maximum(c * nblk - 1, 0), col_blk + off))
    out_spec = pl.BlockSpec((1, nres, chunk, w), lambda b, r, c: (b, r, c, 0))
    shape = jax.ShapeDtypeStruct((bsz, dil, length, w), F32)
    return pl.pallas_call(
        functools.partial(_dil_kernel, nblk=nblk),
        grid=(bsz, dil // nres, length // chunk),
        in_specs=[cur(0), cur(1), prev(1), cur(2), prev(2),
                  pl.BlockSpec((1, DIL_HEADS_PER_GROUP, 2, n, 2 * n), lambda b, r, c: (group, 0, 0, 0, 0))],
        out_specs=[out_spec, out_spec],
        out_shape=[shape, shape],
        compiler_params=_cparams(("parallel", "parallel", "arbitrary")),
        name="dilated_attention",
    )(qkv, qkv, qkv, qkv, qkv, bias)


def _merge_kernel(x_ref, xb_ref, oa_ref, ob_ref, o0_ref, l0_ref, o1_ref, l1_ref, o2_ref, l2_ref,
                  wa_ref, wb_ref, wc_ref, wg_ref, wo_ref, g_ref, b_ref,
                  y_ref, yb_ref, s1o, s1l, s2o, s2l, oc_ref, *, alpha, tm):
    d = D_MODEL

    def combine_head_slot(c):
        cols = slice(c * DIL_DIM, (c + 1) * DIL_DIM)
        for src, dst in ((o1_ref, s1o), (l1_ref, s1l), (o2_ref, s2o), (l2_ref, s2l)):
            dil = src.shape[1]
            rows = tm // dil
            for r in range(dil):
                dst[c, pl.ds(r, rows, stride=dil), :] = src[0, r, :, cols]
        l0 = l0_ref[0, 0, :, cols]
        l1 = s1l[c]
        l2 = s2l[c]
        mx = jnp.maximum(jnp.maximum(l0, l1), l2)
        w0 = jnp.exp2(l0 - mx)
        w1 = jnp.exp2(l1 - mx)
        w2 = jnp.exp2(l2 - mx)
        oc = (w0 * o0_ref[0, 0, :, cols] + w1 * s1o[c] + w2 * s2o[c]) * (1.0 / (w0 + w1 + w2))
        oc_ref[:, cols] = oc.astype(BF16)

    gate_a = _dot(xb_ref[...], wg_ref[:, 0:d])
    combine_head_slot(0)
    branch_a = _dot(oa_ref[...], wa_ref[...])
    combine_head_slot(1)
    gate_b = _dot(xb_ref[...], wg_ref[:, d:2 * d])
    combine_head_slot(2)
    branch_b = _dot(ob_ref[...], wb_ref[...])
    combine_head_slot(3)
    gate_c = _dot(xb_ref[...], wg_ref[:, 2 * d:3 * d])
    merged = _sigmoid(gate_a) * branch_a + _sigmoid(gate_b) * branch_b
    merged = merged + _sigmoid(gate_c) * _dot(oc_ref[...], wc_ref[...])
    mixed = _dot(merged.astype(BF16), wo_ref[...])
    y = _layer_norm(alpha * x_ref[...] + mixed, g_ref[...], b_ref[...])
    y_ref[...] = y
    yb_ref[...] = y.astype(BF16)


def _merge(x, xb, oa, ob, c0, c1, c2, wa, wb, wc, wg, wo, g, b, alpha):
    bsz, seq, d = x.shape
    tm = min(MERGE_TM, seq)
    row = lambda width: pl.BlockSpec((None, tm, width), lambda bb, i: (bb, i, 0))
    full = lambda arr: pl.BlockSpec(arr.shape, lambda bb, i: (0,) * arr.ndim)

    def strided(arr):
        dil = arr.shape[1]
        return pl.BlockSpec((1, dil, tm // dil, DIL_COLS), lambda bb, i: (bb, 0, i, 0))

    weights = (wa, wb, wc, wg, wo, g, b)
    return pl.pallas_call(
        functools.partial(_merge_kernel, alpha=alpha, tm=tm),
        grid=(bsz, seq // tm),
        in_specs=[row(d), row(d), row(DA_COLS), row(SB_COLS),
                  strided(c0[0]), strided(c0[1]), strided(c1[0]), strided(c1[1]),
                  strided(c2[0]), strided(c2[1])] + [full(a) for a in weights],
        out_specs=[row(d), row(d)],
        out_shape=[jax.ShapeDtypeStruct((bsz, seq, d), F32), jax.ShapeDtypeStruct((bsz, seq, d), BF16)],
        scratch_shapes=[pltpu.VMEM((DIL_HEADS_PER_GROUP, tm, DIL_DIM), F32) for _ in range(4)]
        + [pltpu.VMEM((tm, DIL_COLS), BF16)],
        compiler_params=_cparams(("parallel", "arbitrary")),
        name="merge_outproj_ln",
    )(x, xb, oa, ob, c0[0], c0[1], c1[0], c1[1], c2[0], c2[1], *weights)


def _mlp_kernel(x_ref, xb_ref, p_ref, wu_ref, wd_ref, wpg_ref, wp_ref, g_ref, b_ref, y_ref, yb_ref, *, alpha, tf):
    xb = xb_ref[...]
    mlp = None
    for f in range(D_FF // tf):
        hid = jnp.maximum(_dot(xb, wu_ref[:, f * tf:(f + 1) * tf]), 0.0)
        part = _dot((hid * hid).astype(BF16), wd_ref[f * tf:(f + 1) * tf, :])
        mlp = part if mlp is None else mlp + part
    ple = _sigmoid(_dot(xb, wpg_ref[...])) * _dot(p_ref[...].astype(BF16), wp_ref[...])
    y = _layer_norm(alpha * x_ref[...] + mlp + ple, g_ref[...], b_ref[...])
    y_ref[...] = y
    yb_ref[...] = y.astype(BF16)


def _mlp(x, xb, p, wu, wd, wpg, wp, g, b, alpha):
    n, d = x.shape
    tm = min(MLP_TM, n)
    row = lambda width: pl.BlockSpec((tm, width), lambda i: (i, 0))
    full = lambda arr: pl.BlockSpec(arr.shape, lambda i: (0,) * arr.ndim, pipeline_mode=pl.Buffered(1))
    return pl.pallas_call(
        functools.partial(_mlp_kernel, alpha=alpha, tf=MLP_TF),
        grid=(n // tm,),
        in_specs=[row(d), row(d), row(PLE_DIM), full(wu), full(wd), full(wpg), full(wp), full(g), full(b)],
        out_specs=[row(d), row(d)],
        out_shape=[jax.ShapeDtypeStruct((n, d), F32), jax.ShapeDtypeStruct((n, d), BF16)],
        compiler_params=_cparams(("parallel",)),
        name="mlp_ple_ln",
    )(x, xb, p, wu, wd, wpg, wp, g, b)


def kernel(x, p, w_in, da_lambda, da_norm, w_branch_da, w_branch_sb, w_branch_dil, w_out,
           ln1_g, ln1_b, w_up, w_down, w_ple_gate, w_ple, ln2_g, ln2_b, rel_bias):
    bsz, seq, d = x.shape
    depth = w_in.shape[0]
    n = bsz * seq
    alpha = (2 * depth) ** 0.25
    assert d == D_MODEL and all(wd // dl == DIL_KEYS for wd, dl in DIL_PAIRS)
    assert seq % (DIL_KEYS * DIL_PAIRS[-1][1]) == 0 and seq % min(ATT_TILE, seq) == 0

    qkv_a = 3 * DA_COLS
    qkv_b = 3 * SB_COLS
    c0 = qkv_a + qkv_b
    cw = DIL_GROUPS * DIL_COLS

    def dil_cols(g):
        return [w_in[:, :, c0 + t * cw + g * DIL_COLS: c0 + t * cw + (g + 1) * DIL_COLS] for t in range(3)]

    w_main = jnp.concatenate([w_in[:, :, :2 * DA_COLS], w_in[:, :, qkv_a:qkv_a + 2 * SB_COLS]] + dil_cols(0),
                             axis=-1).astype(BF16)
    w_vt = jnp.concatenate([w_in[:, :, 2 * DA_COLS:qkv_a], w_in[:, :, qkv_a + 2 * SB_COLS:c0]],
                           axis=-1).transpose(0, 2, 1).astype(BF16)
    main_c0 = 2 * DA_COLS + 2 * SB_COLS
    ones = lambda width: jnp.ones((width,), F32)
    qscale = lambda width, dh: jnp.full((width,), LOG2E * dh ** -0.5, F32)
    cs_main = jnp.concatenate([qscale(DA_COLS, DA_QK_DIM), ones(DA_COLS), qscale(SB_COLS, SB_DIM), ones(SB_COLS),
                               qscale(DIL_COLS, DIL_DIM), ones(2 * DIL_COLS)]).reshape(1, -1)
    cs_dil = jnp.concatenate([qscale(DIL_COLS, DIL_DIM), ones(2 * DIL_COLS)]).reshape(1, -1)
    w_g1 = jnp.concatenate(dil_cols(1), axis=-1).astype(BF16)
    w_g2 = jnp.concatenate(dil_cols(2), axis=-1).astype(BF16)
    w_gate = w_in[:, :, c0 + 3 * cw:].astype(BF16)
    wa, wb, wc, wo = (w.astype(BF16) for w in (w_branch_da, w_branch_sb, w_branch_dil, w_out))
    wu, wd, wpg, wp = (w.astype(BF16) for w in (w_up, w_down, w_ple_gate, w_ple))

    dmax = max(seq, REL_MAX_DIST + 1)
    buckets = _rel_bucket(jnp.arange(dmax, dtype=jnp.int32))
    thr = jnp.sum(buckets[None, :] < jnp.arange(REL_BUCKETS, dtype=jnp.int32)[:, None], axis=1).astype(jnp.int32)
    tile = min(ATT_TILE, seq)
    da_bias = _da_bias_tiles(thr, rel_bias, seq, tile, min(DA_TQ, seq))
    dil_bias = _dil_bias_tiles(thr, rel_bias)

    xb = x.astype(BF16)
    for i in range(depth):
        lam_init = 0.8 - 0.6 * math.exp(-0.3 * i)
        proj = _proj(xb, w_main[i], cs_main, w_main.shape[-1] // 2)
        vt = _proj_t(xb, w_vt[i], tile)
        proj_g1 = _proj_dil(xb, w_g1[i], cs_dil, DIL_PAIRS[1][1])
        proj_g2 = _proj_dil(xb, w_g2[i], cs_dil, DIL_PAIRS[2][1])
        oa = _da_attention(proj, vt, da_bias, da_lambda[i], da_norm[i].reshape(DA_V_DIM, 1), lam_init)
        ob = _sb_attention(proj, vt)
        cgrp0 = _dil_attention(proj.reshape(bsz, 1, seq, proj.shape[-1]), main_c0 // DIL_COLS, dil_bias, 0)
        cgrp1 = _dil_attention(proj_g1, 0, dil_bias, 1)
        cgrp2 = _dil_attention(proj_g2, 0, dil_bias, 2)
        x1, x1b = _merge(x, xb, oa, ob, cgrp0, cgrp1, cgrp2, wa[i], wb[i], wc[i], w_gate[i], wo[i],
                         ln1_g[i].reshape(1, d), ln1_b[i].reshape(1, d), alpha)
        x2, x2b = _mlp(x1.reshape(n, d), x1b.reshape(n, d), p[i].reshape(n, PLE_DIM), wu[i], wd[i], wpg[i], wp[i],
                       ln2_g[i].reshape(1, d), ln2_b[i].reshape(1, d), alpha)
        x = x2.reshape(bsz, seq, d)
        xb = x2b.reshape(bsz, seq, d)
    return x
```
